```python
import math
import jax
import jax.numpy as jnp
from jax import lax
import numpy as np

D_MODEL = 2048
BATCH = 2
SEQ = 16384
DEPTH = 4
DEC_BATCH = 8
DEC_SEQ = 64
PAST_LEN = 1024

CHUNK = 64
Q_BLOCK = 128
N_SEGMENTS = 16
A_HEADS = 4
A_NOPE = 128
A_ROPE = 64
A_VDIM = 128
Q_LORA = 512
KV_LORA = 512
ROPE_THETA = 10000.0
B_HEADS = 4
B_DIM = 128
B_PREV_CHUNKS = 8
B_PAST = B_PREV_CHUNKS * CHUNK
B_REL_CLIP = 128
C_HEADS = 4
C_DIM = 128
HEAD_DIM_OUT = 128
N_OUT_HEADS = A_HEADS + B_HEADS + C_HEADS
MIX_WIDTH = N_OUT_HEADS * HEAD_DIM_OUT
D_FF = 5632
CONV_W = 3
EPS = 1e-6

B_WIDTH = B_HEADS * B_DIM
C_WIDTH = C_HEADS * C_DIM
IN_WIDTH = Q_LORA + KV_LORA + A_ROPE + 3 * B_WIDTH + 3 * C_WIDTH
IN_SPLITS = [Q_LORA, Q_LORA + KV_LORA, Q_LORA + KV_LORA + A_ROPE, Q_LORA + KV_LORA + A_ROPE + 3 * B_WIDTH]
A_SCALE = 1.0 / math.sqrt(A_NOPE + A_ROPE)
B_SCALE = 1.0 / math.sqrt(B_DIM)
C_SCALE = 1.0 / math.sqrt(C_DIM)

kernel_name = 'hybrid_mla_band_stickbreak_convffn_stream_step'


def rmsnorm(x, g):
    xf = x.astype(jnp.float32)
    y = xf * lax.rsqrt(jnp.mean(xf * xf, axis=-1, keepdims=True) + EPS)
    return (y * g.astype(jnp.float32)).astype(x.dtype)


def rope(x, pos):
    half = x.shape[-1] // 2
    inv = ROPE_THETA ** (-jnp.arange(half, dtype=jnp.float32) / half)
    ang = pos.astype(jnp.float32)[:, None] * inv[None, :]
    ang = ang.reshape(ang.shape[:1] + (1,) * (x.ndim - 3) + (half,))
    cos, sin = jnp.cos(ang), jnp.sin(ang)
    xf = x.astype(jnp.float32)
    x1, x2 = xf[..., :half], xf[..., half:]
    return jnp.concatenate([x1 * cos - x2 * sin, x1 * sin + x2 * cos], axis=-1).astype(x.dtype)


def key_positions(pos, n_past):
    past = pos[0] - n_past + jnp.arange(n_past, dtype=jnp.int32)
    return jnp.concatenate([past, pos])


def sweep_queries(attend, q_parts, q_pos, n_past):
    n_q = q_pos.shape[0]
    if n_q <= Q_BLOCK:
        return attend(q_parts, q_pos, n_past + n_q)
    nb = n_q // Q_BLOCK
    n_seg = min(N_SEGMENTS, nb)
    bounds = [(i * nb) // n_seg for i in range(n_seg + 1)]
    outs = []
    for b0, b1 in zip(bounds[:-1], bounds[1:]):
        nblk = b1 - b0
        lo, hi = b0 * Q_BLOCK, b1 * Q_BLOCK
        blocks = tuple(jnp.moveaxis(t[:, lo:hi].reshape((t.shape[0], nblk, Q_BLOCK) + t.shape[2:]), 1, 0)
                       for t in q_parts)
        out = lax.map(lambda a, e=n_past + hi: attend(a[0], a[1], e),
                      (blocks, q_pos[lo:hi].reshape(nblk, Q_BLOCK)))
        out = jnp.moveaxis(out, 0, 1)
        outs.append(out.reshape((out.shape[0], nblk * Q_BLOCK) + out.shape[3:]))
    return jnp.concatenate(outs, axis=1)


def mla_attend(q, q_pos, k, v, k_pos):
    s = jnp.einsum('bqhd,bkhd->bhqk', q, k).astype(jnp.float32) * A_SCALE
    mask = (k_pos[None, :] // CHUNK) <= (q_pos[:, None] // CHUNK)
    s = jnp.where(mask, s, -jnp.inf)
    e = jnp.exp(s - jnp.max(s, axis=-1, keepdims=True))
    denom = jnp.transpose(jnp.sum(e, axis=-1), (0, 2, 1))[..., None]
    o = jnp.einsum('bhqk,bkhd->bqhd', e.astype(v.dtype), v)
    return (o.astype(jnp.float32) / denom).astype(v.dtype)


def band_attend(q, k, v, q_pos, k_pos, valid, rel_bias):
    s = jnp.einsum('bnqhd,bnkhd->bnhqk', q, k).astype(jnp.float32) * B_SCALE
    rel = jnp.clip(q_pos[:, None] - k_pos[None, :], -B_REL_CLIP, B_REL_CLIP) + B_REL_CLIP
    s = s + rel_bias[:, rel].astype(jnp.float32)[None, None]
    qc, kc = q_pos // CHUNK, k_pos // CHUNK
    band = (kc[None, :] <= qc[:, None]) & (kc[None, :] >= qc[:, None] - B_PREV_CHUNKS)
    mask = band[None, None, None] & valid[None, :, None, None, :]
    p = jax.nn.softmax(jnp.where(mask, s, -jnp.inf), axis=-1)
    return jnp.einsum('bnhqk,bnkhd->bnqhd', p.astype(v.dtype), v)


def band_prompt(q, k, v, rel_bias):
    n_b, n_t, n_h, d = q.shape
    nc = n_t // CHUNK
    idx = jnp.arange(nc)[:, None] + jnp.arange(B_PREV_CHUNKS + 1)[None, :]

    def gather_band(t):
        tc = jnp.pad(t.reshape(n_b, nc, CHUNK, n_h, d), ((0, 0), (B_PREV_CHUNKS, 0), (0, 0), (0, 0), (0, 0)))
        return tc[:, idx].reshape(n_b, nc, (B_PREV_CHUNKS + 1) * CHUNK, n_h, d)

    q_pos = B_PAST + jnp.arange(CHUNK, dtype=jnp.int32)
    k_pos = jnp.arange((B_PREV_CHUNKS + 1) * CHUNK, dtype=jnp.int32)
    valid = jnp.repeat(idx >= B_PREV_CHUNKS, CHUNK, axis=1)
    out = band_attend(q.reshape(n_b, nc, CHUNK, n_h, d), gather_band(k), gather_band(v), q_pos, k_pos, valid, rel_bias)
    return out.reshape(n_b, n_t, n_h, d)


def band_sample(q, k, v, k_past, v_past, pos, rel_bias):
    k_pos = key_positions(pos, k_past.shape[1])
    k_all = jnp.concatenate([k_past, k], axis=1)[:, None]
    v_all = jnp.concatenate([v_past, v], axis=1)[:, None]
    valid = (k_pos >= 0)[None]
    return band_attend(q[:, None], k_all, v_all, pos, k_pos, valid, rel_bias)[:, 0]


def rev_cumsum_keys(L):
    n_k = L.shape[-1]
    pad = (-n_k) % Q_BLOCK
    Lp = jnp.pad(L, [(0, 0)] * (L.ndim - 1) + [(0, pad)])
    nk = Lp.shape[-1] // Q_BLOCK
    Lb = Lp.reshape(L.shape[:-1] + (nk, Q_BLOCK))
    ar = jnp.arange(Q_BLOCK)
    tri = (ar[None, :] >= ar[:, None]).astype(L.dtype)
    within = jnp.einsum('...nj,sj->...ns', Lb, tri, precision=lax.Precision.HIGHEST)
    an = jnp.arange(nk)
    tri_n = (an[None, :] > an[:, None]).astype(L.dtype)
    after = jnp.einsum('...n,mn->...m', jnp.sum(Lb, axis=-1), tri_n, precision=lax.Precision.HIGHEST)
    return (within + after[..., None]).reshape(Lp.shape)[..., :n_k]


def sb_attend(q, q_pos, k, v, k_pos):
    z = jnp.einsum('bqhd,bkhd->bhqk', q, k).astype(jnp.float32) * C_SCALE
    causal = k_pos[None, :] < q_pos[:, None]
    log_keep = jnp.where(causal, jax.nn.log_sigmoid(-z), 0.0)
    log_w = z + rev_cumsum_keys(log_keep)
    w = jnp.exp(jnp.where(causal, log_w, -jnp.inf))
    return jnp.einsum('bhqk,bkhd->bqhd', w.astype(v.dtype), v)


def conv_ffn(h, conv_past, w_up, w_conv, b_conv, w_down):
    n_t = h.shape[1]
    u = jnp.concatenate([conv_past, h @ w_up], axis=1)
    c = b_conv
    for i in range(CONV_W):
        c = c + w_conv[i] * u[:, i:i + n_t]
    a, g = jnp.split(c, 2, axis=-1)
    y = (jax.nn.silu(a) * g) @ w_down
    return y, u[:, u.shape[1] - (CONV_W - 1):]


def layer(x, pos, ckv_past, kr_past, bk_past, bv_past, ck_past, cv_past, conv_past,
          g_attn, w_in, g_q, w_uq, g_kv, w_uk, w_uv, rel_bias, g_heads, w_out,
          g_ffn, w_up, w_conv, b_conv, w_down, prompt):
    n_b, n_t, _ = x.shape
    h = rmsnorm(x, g_attn)
    cq, ckv_raw, kr_raw, qkv_b, qkv_c = jnp.split(h @ w_in, IN_SPLITS, axis=-1)

    q_a = (rmsnorm(cq, g_q) @ w_uq).reshape(n_b, n_t, A_HEADS, A_NOPE + A_ROPE)
    q_a = jnp.concatenate([q_a[..., :A_NOPE], rope(q_a[..., A_NOPE:], pos)], axis=-1)
    ckv = rmsnorm(ckv_raw, g_kv)
    krope = rope(kr_raw, pos)
    ckv_all = jnp.concatenate([ckv_past, ckv], axis=1)
    kr_all = jnp.concatenate([kr_past, krope], axis=1)
    n_past = ckv_past.shape[1]
    n_k = ckv_all.shape[1]
    ka_pos = key_positions(pos, n_past)
    k_a = jnp.concatenate([(ckv_all @ w_uk).reshape(n_b, n_k, A_HEADS, A_NOPE),
                           jnp.broadcast_to(kr_all[:, :, None], (n_b, n_k, A_HEADS, A_ROPE))], axis=-1)
    v_a = (ckv_all @ w_uv).reshape(n_b, n_k, A_HEADS, A_VDIM)
    out_a = sweep_queries(lambda qs, qp, e: mla_attend(qs[0], qp, k_a[:, :e], v_a[:, :e], ka_pos[:e]),
                          (q_a,), pos, n_past)

    qkv_b = qkv_b.reshape(n_b, n_t, 3, B_HEADS, B_DIM)
    q_b, k_b, v_b = qkv_b[:, :, 0], qkv_b[:, :, 1], qkv_b[:, :, 2]
    if prompt:
        out_b = band_prompt(q_b, k_b, v_b, rel_bias)
        rows = min(B_PAST, n_t)
        bk_new, bv_new = k_b[:, n_t - rows:], v_b[:, n_t - rows:]
    else:
        out_b = band_sample(q_b, k_b, v_b, bk_past, bv_past, pos, rel_bias)
        bk_new, bv_new = k_b, v_b

    qkv_c = qkv_c.reshape(n_b, n_t, 3, C_HEADS, C_DIM)
    q_c, k_c, v_c = qkv_c[:, :, 0], qkv_c[:, :, 1], qkv_c[:, :, 2]
    k_c_all = jnp.concatenate([ck_past, k_c], axis=1)
    v_c_all = jnp.concatenate([cv_past, v_c], axis=1)
    nc_past = ck_past.shape[1]
    kc_pos = key_positions(pos, nc_past)
    out_c = sweep_queries(lambda qs, qp, e: sb_attend(qs[0], qp, k_c_all[:, :e], v_c_all[:, :e], kc_pos[:e]),
                          (q_c,), pos, nc_past)

    heads = jnp.concatenate([out_a, out_b, out_c], axis=2)
    mixed = rmsnorm(heads, g_heads.reshape(N_OUT_HEADS, HEAD_DIM_OUT)).reshape(n_b, n_t, MIX_WIDTH)
    x = x + mixed @ w_out
    y, conv_new = conv_ffn(rmsnorm(x, g_ffn), conv_past, w_up, w_conv, b_conv, w_down)
    x = x + y
    return x, ckv, krope, bk_new, bv_new, k_c, v_c, conv_new


def setup_inputs(seed: int = 0) -> dict:
    key = jax.random.key(seed)
    ks = jax.random.split(key, 32)
    f32 = jnp.float32

    def nrm(k, shape, scale=1.0):
        return scale * jax.random.normal(k, shape, f32)

    def gain(k, shape):
        return 1.0 + 0.02 * jax.random.normal(k, shape, f32)

    r_b = min(B_PAST, PAST_LEN)
    return {
        'x_prompt': nrm(ks[0], (BATCH, SEQ, D_MODEL)),
        'x_sample': nrm(ks[1], (DEC_BATCH, DEC_SEQ, D_MODEL)),
        'cache_a_ckv': nrm(ks[2], (DEPTH, DEC_BATCH, PAST_LEN, KV_LORA)),
        'cache_a_krope': nrm(ks[3], (DEPTH, DEC_BATCH, PAST_LEN, A_ROPE)),
        'cache_b_k': nrm(ks[4], (DEPTH, DEC_BATCH, r_b, B_HEADS, B_DIM)),
        'cache_b_v': nrm(ks[5], (DEPTH, DEC_BATCH, r_b, B_HEADS, B_DIM)),
        'cache_c_k': nrm(ks[6], (DEPTH, DEC_BATCH, PAST_LEN, C_HEADS, C_DIM)),
        'cache_c_v': nrm(ks[7], (DEPTH, DEC_BATCH, PAST_LEN, C_HEADS, C_DIM)),
        'state_conv': nrm(ks[8], (DEPTH, DEC_BATCH, CONV_W - 1, 2 * D_FF)),
        'g_attn': gain(ks[9], (DEPTH, D_MODEL)),
        'w_in': nrm(ks[10], (DEPTH, D_MODEL, IN_WIDTH), D_MODEL ** -0.5),
        'g_q': gain(ks[11], (DEPTH, Q_LORA)),
        'w_uq': nrm(ks[12], (DEPTH, Q_LORA, A_HEADS * (A_NOPE + A_ROPE)), Q_LORA ** -0.5),
        'g_kv': gain(ks[13], (DEPTH, KV_LORA)),
        'w_uk': nrm(ks[14], (DEPTH, KV_LORA, A_HEADS * A_NOPE), KV_LORA ** -0.5),
        'w_uv': nrm(ks[15], (DEPTH, KV_LORA, A_HEADS * A_VDIM), KV_LORA ** -0.5),
        'rel_bias': nrm(ks[16], (DEPTH, B_HEADS, 2 * B_REL_CLIP + 1), 0.2),
        'g_heads': gain(ks[17], (DEPTH, MIX_WIDTH)),
        'w_out': nrm(ks[18], (DEPTH, MIX_WIDTH, D_MODEL), MIX_WIDTH ** -0.5),
        'g_ffn': gain(ks[19], (DEPTH, D_MODEL)),
        'w_up': nrm(ks[20], (DEPTH, D_MODEL, 2 * D_FF), D_MODEL ** -0.5),
        'w_conv': nrm(ks[21], (DEPTH, CONV_W, 2 * D_FF), CONV_W ** -0.5),
        'b_conv': nrm(ks[22], (DEPTH, 2 * D_FF), 0.01),
        'w_down': nrm(ks[23], (DEPTH, D_FF, D_MODEL), D_FF ** -0.5),
        'g_final': gain(ks[24], (D_MODEL,)),
    }


def reference(x_prompt, x_sample, cache_a_ckv, cache_a_krope, cache_b_k, cache_b_v, cache_c_k, cache_c_v,
              state_conv, g_attn, w_in, g_q, w_uq, g_kv, w_uk, w_uv, rel_bias, g_heads, w_out,
              g_ffn, w_up, w_conv, b_conv, w_down, g_final):
    n_bp, n_tp, _ = x_prompt.shape
    n_ts = x_sample.shape[1]
    past_len = cache_c_k.shape[2]
    dt = x_prompt.dtype
    pos_p = jnp.arange(n_tp, dtype=jnp.int32)
    pos_s = past_len + jnp.arange(n_ts, dtype=jnp.int32)
    empty_ckv = jnp.zeros((n_bp, 0, KV_LORA), dt)
    empty_kr = jnp.zeros((n_bp, 0, A_ROPE), dt)
    empty_c = jnp.zeros((n_bp, 0, C_HEADS, C_DIM), dt)
    conv_zero = jnp.zeros((n_bp, CONV_W - 1, 2 * D_FF), dt)

    xp, xs = x_prompt, x_sample
    p_layers, s_layers = [], []
    for l in range(DEPTH):
        weights = (g_attn[l], w_in[l], g_q[l], w_uq[l], g_kv[l], w_uk[l], w_uv[l], rel_bias[l], g_heads[l],
                   w_out[l], g_ffn[l], w_up[l], w_conv[l], b_conv[l], w_down[l])
        xp, *p_st = layer(xp, pos_p, empty_ckv, empty_kr, None, None, empty_c, empty_c, conv_zero,
                          *weights, prompt=True)
        xs, *s_st = layer(xs, pos_s, cache_a_ckv[l], cache_a_krope[l], cache_b_k[l], cache_b_v[l],
                          cache_c_k[l], cache_c_v[l], state_conv[l], *weights, prompt=False)
        p_layers.append(p_st)
        s_layers.append(s_st)

    y_prompt = rmsnorm(xp, g_final)
    y_sample = rmsnorm(xs, g_final)
    p_a_ckv, p_a_krope, p_b_k, p_b_v, p_c_k, p_c_v, p_conv = [jnp.stack(t) for t in zip(*p_layers)]
    s_a_ckv, s_a_krope, s_b_k, s_b_v, s_c_k, s_c_v, s_conv = [jnp.stack(t) for t in zip(*s_layers)]
    return (y_prompt, y_sample, p_a_ckv, p_a_krope, p_b_k, p_b_v, p_c_k, p_c_v, p_conv,
            s_a_ckv, s_a_krope, s_b_k, s_b_v, s_c_k, s_c_v, s_conv)
```

```python
import functools
import math

import numpy as np
import jax
import jax.numpy as jnp
from jax import lax
from jax.experimental import pallas as pl
from jax.experimental.pallas import tpu as pltpu

F32 = jnp.float32
BF16 = jnp.bfloat16

EPS = 1e-6
CHUNK = 64
ROPE_THETA = 10000.0
A_HEADS, A_NOPE, A_ROPE, A_VDIM = 4, 128, 64, 128
A_QK_PAD = 256
B_HEADS, B_DIM, B_PREV_CHUNKS, B_REL_CLIP = 4, 128, 8, 128
B_PAST = B_PREV_CHUNKS * CHUNK
C_HEADS, C_DIM = 4, 128
HEAD_DIM = 128
CONV_W = 3
LOG2E = 1.4426950408889634
A_SCALE = 1.0 / math.sqrt(A_NOPE + A_ROPE)
B_SCALE = 1.0 / math.sqrt(B_DIM)
C_SCALE = 1.0 / math.sqrt(C_DIM)

SB_EXIT_LOG2 = 160.0

B_QTILE = 4 * CHUNK
B_WIN = (B_PREV_CHUNKS + 4) * CHUNK

VMEM_LIMIT = 56 * 1024 * 1024


def _cparams(sem):
    return pltpu.CompilerParams(dimension_semantics=sem, vmem_limit_bytes=VMEM_LIMIT)


def _rms_rows(x, g):
    ms = jnp.mean(x * x, axis=-1, keepdims=True)
    return x * lax.rsqrt(ms + EPS) * g


def _rms_cols(o, g):
    ms = jnp.mean(o * o, axis=0, keepdims=True)
    return o * lax.rsqrt(ms + EPS) * g


_NT = (((1,), (1,)), ((), ()))
_TN = (((0,), (0,)), ((), ()))


def _resident(shape, index_map):
    return pl.BlockSpec(shape, index_map, pipeline_mode=pl.Buffered(1))


IN_COLS = 512 + 512 + 256 + 1536 + 1536


def _in_proj_kernel(x_ref, cos_ref, sin_ref, gattn_ref, gq_ref, gkv_ref, w_ref,
                    hq_ref, ckv_ref, ckvb_ref, kr_ref, krp_ref,
                    qb_ref, kb_ref, kbb_ref, vb_ref, vbt_ref,
                    qc_ref, kc_ref, kcp_ref, vc_ref, vctp_ref, ks_ref, vs_ref, *, tkc):
    h = _rms_rows(x_ref[...], gattn_ref[...]).astype(BF16)

    def proj(lo, n):
        return jnp.dot(h, w_ref[:, lo:lo + n], preferred_element_type=F32)

    hq_ref[...] = _rms_rows(proj(0, 512), gq_ref[...]).astype(BF16)
    ckv = _rms_rows(proj(512, 512), gkv_ref[...])
    ckv_ref[...] = ckv
    ckvb_ref[...] = ckv.astype(BF16)
    kr2 = proj(1024, 256)
    krp = kr2[:, :128] * cos_ref[...] + kr2[:, 128:] * sin_ref[...]
    krp_ref[...] = krp.astype(BF16)
    kr_ref[...] = krp[:, :A_ROPE]
    o = 1280
    qb_ref[...] = (proj(o, 512) * (B_SCALE * LOG2E)).astype(BF16)
    kb = proj(o + 512, 512)
    kb_ref[...] = kb
    kbb_ref[...] = kb.astype(BF16)
    vb = proj(o + 1024, 512)
    vb_ref[...] = vb
    vbt_ref[...] = vb.T.astype(BF16)
    o = 2816
    qc_ref[...] = (proj(o, 512) * (C_SCALE * LOG2E)).astype(BF16)
    kc = proj(o + 512, 512)
    vc = proj(o + 1024, 512)
    kc_ref[...] = kc
    vc_ref[...] = vc
    tm = kc_ref.shape[0]
    m = tkc // 8
    for hd in range(C_HEADS):
        cols = slice(hd * C_DIM, (hd + 1) * C_DIM)
        ks_ref[hd] = kc[:, cols]
        vs_ref[hd] = vc[:, cols]
        for t0 in range(0, tm, tkc):
            kp = jnp.concatenate([ks_ref[hd, pl.ds(t0 + i, 8, stride=m), :] for i in range(m)], axis=0)
            kcp_ref[t0:t0 + tkc, cols] = kp.astype(BF16)
            vp = jnp.concatenate([vs_ref[hd, pl.ds(t0 + i, 8, stride=m), :] for i in range(m)], axis=0)
            vctp_ref[cols, t0:t0 + tkc] = vp.T.astype(BF16)


def _in_proj(x, cos_t, sin_t, g_attn, g_q, g_kv, w_ext, *, tm, tkc):
    nb, s, d = x.shape
    grid = (nb, s // tm)
    tok = lambda c, dt: jax.ShapeDtypeStruct((nb, s, c), dt)
    tr = lambda c: jax.ShapeDtypeStruct((nb, c, s), BF16)
    out_shape = (tok(512, BF16), tok(512, F32), tok(512, BF16), tok(A_ROPE, F32), tok(128, BF16),
                 tok(512, BF16), tok(512, F32), tok(512, BF16), tok(512, F32), tr(512),
                 tok(512, BF16), tok(512, F32), tok(512, BF16), tok(512, F32), tr(512))
    tspec = lambda c: pl.BlockSpec((None, tm, c), lambda b, i: (b, i, 0))
    trspec = pl.BlockSpec((None, 512, tm), lambda b, i: (b, 0, i))
    out_specs = (tspec(512), tspec(512), tspec(512), tspec(A_ROPE), tspec(128),
                 tspec(512), tspec(512), tspec(512), tspec(512), trspec,
                 tspec(512), tspec(512), tspec(512), tspec(512), trspec)
    vec = lambda n: _resident((1, n), lambda b, i: (0, 0))
    in_specs = [tspec(d),
                pl.BlockSpec((tm, 128), lambda b, i: (i, 0)),
                pl.BlockSpec((tm, 128), lambda b, i: (i, 0)),
                vec(d), vec(512), vec(512),
                _resident((d, IN_COLS), lambda b, i: (0, 0))]
    return pl.pallas_call(
        functools.partial(_in_proj_kernel, tkc=tkc),
        out_shape=out_shape, grid=grid, in_specs=in_specs, out_specs=out_specs,
        scratch_shapes=[pltpu.VMEM((C_HEADS, tm, C_DIM), F32), pltpu.VMEM((C_HEADS, tm, C_DIM), F32)],
        compiler_params=_cparams(("parallel", "parallel")), name="in_proj",
    )(x, cos_t, sin_t, g_attn, g_q, g_kv, w_ext)


def _q_proj_a_kernel(hq_ref, cos_ref, sin_ref, w_ref, q_ref):
    ql = jnp.dot(hq_ref[...], w_ref[...], preferred_element_type=F32)
    cos, sin = cos_ref[...], sin_ref[...]
    sc = A_SCALE * LOG2E
    for hd in range(A_HEADS):
        lo = hd * A_QK_PAD
        rot = ql[:, lo + 128:lo + 256] * cos + ql[:, 1024 + hd * 128:1024 + (hd + 1) * 128] * sin
        q_ref[:, lo:lo + 128] = (ql[:, lo:lo + 128] * sc).astype(BF16)
        q_ref[:, lo + 128:lo + 256] = (rot * sc).astype(BF16)


def _q_proj_a(hq, cos_t, sin_t, w_uq_ext, *, tm):
    nb, s, _ = hq.shape
    return pl.pallas_call(
        _q_proj_a_kernel,
        out_shape=jax.ShapeDtypeStruct((nb, s, A_HEADS * A_QK_PAD), BF16),
        grid=(nb, s // tm),
        in_specs=[pl.BlockSpec((None, tm, 512), lambda b, i: (b, i, 0)),
                  pl.BlockSpec((tm, 128), lambda b, i: (i, 0)),
                  pl.BlockSpec((tm, 128), lambda b, i: (i, 0)),
                  _resident((512, 1536), lambda b, i: (0, 0))],
        out_specs=pl.BlockSpec((None, tm, A_HEADS * A_QK_PAD), lambda b, i: (b, i, 0)),
        compiler_params=_cparams(("parallel", "parallel")), name="q_proj_a",
    )(hq, cos_t, sin_t, w_uq_ext)


def _kv_proj_a_kernel(ckv_ref, krp_ref, wk_ref, wv_ref, k_ref, vt_ref):
    c = ckv_ref[...]
    kn = jnp.dot(c, wk_ref[...], preferred_element_type=F32)
    v = jnp.dot(c, wv_ref[...], preferred_element_type=F32)
    krp = krp_ref[...]
    for hd in range(A_HEADS):
        lo = hd * A_QK_PAD
        k_ref[:, lo:lo + 128] = kn[:, hd * 128:(hd + 1) * 128].astype(BF16)
        k_ref[:, lo + 128:lo + 256] = krp
    vt_ref[...] = v.T.astype(BF16)


def _kv_proj_a(ckvb, krp, w_uk, w_uv, *, tm):
    nb, s, _ = ckvb.shape
    return pl.pallas_call(
        _kv_proj_a_kernel,
        out_shape=(jax.ShapeDtypeStruct((nb, s, A_HEADS * A_QK_PAD), BF16),
                   jax.ShapeDtypeStruct((nb, A_HEADS * A_VDIM, s), BF16)),
        grid=(nb, s // tm),
        in_specs=[pl.BlockSpec((None, tm, 512), lambda b, i: (b, i, 0)),
                  pl.BlockSpec((None, tm, 128), lambda b, i: (b, i, 0)),
                  _resident((512, 512), lambda b, i: (0, 0)),
                  _resident((512, 512), lambda b, i: (0, 0))],
        out_specs=(pl.BlockSpec((None, tm, A_HEADS * A_QK_PAD), lambda b, i: (b, i, 0)),
                   pl.BlockSpec((None, A_HEADS * A_VDIM, tm), lambda b, i: (b, 0, i))),
        compiler_params=_cparams(("parallel", "parallel")), name="kv_proj_a",
    )(ckvb, krp, w_uk, w_uv)


def _attn_a_kernel(q_ref, k_ref, vt_ref, g_ref, o_ref, m_ref, l_ref, acc_ref, *, tq, tk, n_past):
    qi = pl.program_id(2)
    q = q_ref[...]
    q0 = n_past + qi * tq
    n_full = (q0 + CHUNK) // tk
    n_kb = (q0 + tq + tk - 1) // tk
    m_ref[...] = jnp.full(m_ref.shape, -jnp.inf, F32)
    l_ref[...] = jnp.zeros(l_ref.shape, F32)
    acc_ref[...] = jnp.zeros(acc_ref.shape, F32)

    def step(j, masked):
        start = pl.multiple_of(j * tk, tk)
        s = lax.dot_general(k_ref[pl.ds(start, tk), :], q, _NT, preferred_element_type=F32)
        if masked:
            kpos = start + lax.broadcasted_iota(jnp.int32, (tk, tq), 0)
            qpos = q0 + lax.broadcasted_iota(jnp.int32, (tk, tq), 1)
            s = jnp.where((kpos >> 6) <= (qpos >> 6), s, -jnp.inf)
        m_old = m_ref[...]
        m_new = jnp.maximum(m_old, jnp.max(s, axis=0, keepdims=True))
        p = jnp.exp2(s - m_new)
        alpha = jnp.exp2(m_old - m_new)
        l_ref[...] = alpha * l_ref[...] + jnp.sum(p, axis=0, keepdims=True)
        pv = jnp.dot(vt_ref[:, pl.ds(start, tk)], p.astype(BF16), preferred_element_type=F32)
        acc_ref[...] = alpha * acc_ref[...] + pv
        m_ref[...] = m_new

    def full_body(j, c):
        step(j, False)
        return c

    def masked_body(j, c):
        step(j, True)
        return c

    lax.fori_loop(0, n_full, full_body, 0)
    lax.fori_loop(n_full, n_kb, masked_body, 0)
    o = acc_ref[...] / l_ref[...]
    o_ref[...] = _rms_cols(o, g_ref[...]).astype(BF16)


def _attn_a(q, k, vt, g_col, *, tq, tk, n_past):
    nb, sq, _ = q.shape
    sk = k.shape[1]
    return pl.pallas_call(
        functools.partial(_attn_a_kernel, tq=tq, tk=tk, n_past=n_past),
        out_shape=jax.ShapeDtypeStruct((nb, A_HEADS * A_VDIM, sq), BF16),
        grid=(nb, A_HEADS, sq // tq),
        in_specs=[pl.BlockSpec((None, tq, A_QK_PAD), lambda b, h, i: (b, i, h)),
                  pl.BlockSpec((None, sk, A_QK_PAD), lambda b, h, i: (b, 0, h)),
                  pl.BlockSpec((None, A_VDIM, sk), lambda b, h, i: (b, h, 0)),
                  pl.BlockSpec((A_VDIM, 1), lambda b, h, i: (h, 0))],
        out_specs=pl.BlockSpec((None, A_VDIM, tq), lambda b, h, i: (b, h, i)),
        scratch_shapes=[pltpu.VMEM((1, tq), F32), pltpu.VMEM((1, tq), F32), pltpu.VMEM((A_VDIM, tq), F32)],
        compiler_params=_cparams(("parallel", "parallel", "arbitrary")), name="attn_a",
    )(q, k, vt, g_col)


def _attn_b_kernel(q_ref, k_ref, vt_ref, bias_ref, g_ref, o_ref, *, hist0):
    st = pl.program_id(2)
    start = pl.multiple_of(st * B_QTILE, B_QTILE)
    s = lax.dot_general(k_ref[pl.ds(start, B_WIN), :], q_ref[...], _NT, preferred_element_type=F32)
    s = s + bias_ref[...]
    thr = B_PAST - hist0 - st * B_QTILE
    row = lax.broadcasted_iota(jnp.int32, (B_WIN, B_QTILE), 0)
    s = jnp.where(row >= thr, s, -jnp.inf)
    m = jnp.max(s, axis=0, keepdims=True)
    p = jnp.exp2(s - m)
    l = jnp.sum(p, axis=0, keepdims=True)
    o = jnp.dot(vt_ref[:, pl.ds(start, B_WIN)], p.astype(BF16), preferred_element_type=F32) / l
    o_ref[...] = _rms_cols(o, g_ref[...]).astype(BF16)


def _attn_b(q, k_pad, vt_pad, bias, g_col, *, hist0):
    nb, sq, _ = q.shape
    skp = k_pad.shape[1]
    return pl.pallas_call(
        functools.partial(_attn_b_kernel, hist0=hist0),
        out_shape=jax.ShapeDtypeStruct((nb, B_HEADS * B_DIM, sq), BF16),
        grid=(nb, B_HEADS, sq // B_QTILE),
        in_specs=[pl.BlockSpec((None, B_QTILE, B_DIM), lambda b, h, i: (b, i, h)),
                  pl.BlockSpec((None, skp, B_DIM), lambda b, h, i: (b, 0, h)),
                  pl.BlockSpec((None, B_DIM, skp), lambda b, h, i: (b, h, 0)),
                  pl.BlockSpec((None, B_WIN, B_QTILE), lambda b, h, i: (h, 0, 0)),
                  pl.BlockSpec((B_DIM, 1), lambda b, h, i: (h, 0))],
        out_specs=pl.BlockSpec((None, B_DIM, B_QTILE), lambda b, h, i: (b, h, i)),
        compiler_params=_cparams(("parallel", "parallel", "arbitrary")), name="attn_b",
    )(q, k_pad, vt_pad, bias, g_col)


def _band_bias_table(rel_bias):
    kk = np.arange(B_WIN)[:, None]
    qq = np.arange(B_QTILE)[None, :]
    idx = np.clip(B_PAST + qq - kk, -B_REL_CLIP, B_REL_CLIP) + B_REL_CLIP
    kc, qc = kk // CHUNK, qq // CHUNK
    valid = (kc >= qc) & (kc <= qc + B_PREV_CHUNKS)
    tbl = rel_bias[:, idx] * LOG2E
    return jnp.where(valid[None], tbl, -jnp.inf)


def _attn_c_kernel(q_ref, kp_ref, vtp_ref, g_ref, o_ref, c_ref, acc_ref, *, tq, tk, n_past):
    qi = pl.program_id(2)
    q = q_ref[...]
    q0 = n_past + qi * tq
    n_full = q0 // tk
    n_mask = tq // tk
    m = tk // 8
    c_ref[...] = jnp.zeros(c_ref.shape, F32)
    acc_ref[...] = jnp.zeros(acc_ref.shape, F32)
    sub = lax.broadcasted_iota(jnp.int32, (8, tq), 0)

    def tile(j, masked):
        start = pl.multiple_of(j * tk, tk)
        z = lax.dot_general(kp_ref[pl.ds(start, tk), :], q, _NT, preferred_element_type=F32)
        sp = jnp.maximum(z, 0.0) + jnp.log2(1.0 + jnp.exp2(-jnp.abs(z)))
        if masked:
            r = lax.broadcasted_iota(jnp.int32, (tk, tq), 0)
            kpos = start + (r & 7) * m + (r >> 3)
            qpos = q0 + lax.broadcasted_iota(jnp.int32, (tk, tq), 1)
            causal = kpos < qpos
            sp = jnp.where(causal, sp, 0.0)
        run = [None] * m
        run[m - 1] = sp[(m - 1) * 8:m * 8]
        for i in range(m - 2, -1, -1):
            run[i] = sp[i * 8:(i + 1) * 8] + run[i + 1]
        tot = run[0]
        incl = tot
        for sh in (1, 2, 4):
            incl = incl + jnp.where(sub + sh < 8, pltpu.roll(incl, 8 - sh, 0), 0.0)
        c_in = c_ref[...]
        off = (incl - tot) + c_in
        rows = []
        for i in range(m):
            w = jnp.exp2(z[i * 8:(i + 1) * 8] - (run[i] + off))
            if masked:
                w = jnp.where(causal[i * 8:(i + 1) * 8], w, 0.0)
            rows.append(w)
        w = jnp.concatenate(rows, axis=0).astype(BF16)
        acc_ref[...] += jnp.dot(vtp_ref[:, pl.ds(start, tk)], w, preferred_element_type=F32)
        c_new = c_in + incl[0:1, :]
        c_ref[...] = c_new
        return jnp.min(c_new)

    cmin = jnp.float32(0.0)
    for jj in range(n_mask - 1, -1, -1):
        cmin = tile(n_full + jj, True)

    def cond(carry):
        j, cm = carry
        return jnp.logical_and(j >= 0, cm < SB_EXIT_LOG2)

    def body(carry):
        j, _ = carry
        return j - 1, tile(j, False)

    lax.while_loop(cond, body, (n_full - 1, cmin))
    o_ref[...] = _rms_cols(acc_ref[...], g_ref[...]).astype(BF16)


def _attn_c(q, kp, vtp, g_col, *, tq, tk, n_past):
    nb, sq, _ = q.shape
    sk = kp.shape[1]
    return pl.pallas_call(
        functools.partial(_attn_c_kernel, tq=tq, tk=tk, n_past=n_past),
        out_shape=jax.ShapeDtypeStruct((nb, C_HEADS * C_DIM, sq), BF16),
        grid=(nb, C_HEADS, sq // tq),
        in_specs=[pl.BlockSpec((None, tq, C_DIM), lambda b, h, i: (b, i, h)),
                  pl.BlockSpec((None, sk, C_DIM), lambda b, h, i: (b, 0, h)),
                  pl.BlockSpec((None, C_DIM, sk), lambda b, h, i: (b, h, 0)),
                  pl.BlockSpec((C_DIM, 1), lambda b, h, i: (h, 0))],
        out_specs=pl.BlockSpec((None, C_DIM, tq), lambda b, h, i: (b, h, i)),
        scratch_shapes=[pltpu.VMEM((1, tq), F32), pltpu.VMEM((C_DIM, tq), F32)],
        compiler_params=_cparams(("parallel", "parallel", "arbitrary")), name="attn_c",
    )(q, kp, vtp, g_col)


def _permute_keys(t, tk):
    nb, s, c = t.shape
    m = tk // 8
    return t.reshape(nb, s // tk, 8, m, c).transpose(0, 1, 3, 2, 4).reshape(nb, s, c)


def _out_proj_kernel(ma_ref, mb_ref, mc_ref, x_ref, w_ref, g_ref, x2_ref, h2_ref):
    y = lax.dot_general(ma_ref[...], w_ref[0:512, :], _TN, preferred_element_type=F32)
    y = y + lax.dot_general(mb_ref[...], w_ref[512:1024, :], _TN, preferred_element_type=F32)
    y = y + lax.dot_general(mc_ref[...], w_ref[1024:1536, :], _TN, preferred_element_type=F32)
    x2 = x_ref[...] + y
    x2_ref[...] = x2
    h2_ref[...] = _rms_rows(x2, g_ref[...]).astype(BF16)


def _out_proj(ma, mb, mc, x, w_out, g_ffn, *, tm):
    nb, s, d = x.shape
    mspec = pl.BlockSpec((None, 512, tm), lambda b, i: (b, 0, i))
    xspec = pl.BlockSpec((None, tm, d), lambda b, i: (b, i, 0))
    return pl.pallas_call(
        _out_proj_kernel,
        out_shape=(jax.ShapeDtypeStruct((nb, s, d), F32), jax.ShapeDtypeStruct((nb, s, d), BF16)),
        grid=(nb, s // tm),
        in_specs=[mspec, mspec, mspec, xspec,
                  _resident((1536, d), lambda b, i: (0, 0)),
                  _resident((1, d), lambda b, i: (0, 0))],
        out_specs=(xspec, xspec),
        compiler_params=_cparams(("parallel", "parallel")), name="out_proj",
    )(ma, mb, mc, x, w_out, g_ffn)


def _ffn_kernel(h_ref, x_ref, wa_ref, wg_ref, wca_ref, wcg_ref, ba_ref, bg_ref, wd_ref, pa_ref, pg_ref,
                o_ref, ca_ref, cg_ref, carry_ref, *, tm, fc):
    i = pl.program_id(1)
    f = pl.program_id(2)

    @pl.when(f == 0)
    def _():
        o_ref[...] = x_ref[...]

    @pl.when(i == 0)
    def _():
        carry_ref[f, 0:2, :] = pa_ref[...]
        carry_ref[f, 2:4, :] = pg_ref[...]

    h = h_ref[...]
    row = lax.broadcasted_iota(jnp.int32, (tm, fc), 0)

    def conv(u, prev, wc_ref, b_ref):
        u1 = jnp.where(row == 0, prev[1:2, :], pltpu.roll(u, 1, 0))
        u2 = jnp.where(row == 0, prev[0:1, :], jnp.where(row == 1, prev[1:2, :], pltpu.roll(u, 2, 0)))
        return b_ref[...] + wc_ref[0:1, :] * u2 + wc_ref[1:2, :] * u1 + wc_ref[2:3, :] * u

    ua = jnp.dot(h, wa_ref[...], preferred_element_type=F32)
    ug = jnp.dot(h, wg_ref[...], preferred_element_type=F32)
    a = conv(ua, carry_ref[f, 0:2, :], wca_ref, ba_ref)
    g = conv(ug, carry_ref[f, 2:4, :], wcg_ref, bg_ref)
    act = a * (1.0 / (1.0 + jnp.exp(-a))) * g
    o_ref[...] += jnp.dot(act.astype(BF16), wd_ref[...], preferred_element_type=F32)
    last_a = ua[tm - 2:tm, :]
    last_g = ug[tm - 2:tm, :]
    carry_ref[f, 0:2, :] = last_a
    carry_ref[f, 2:4, :] = last_g
    ca_ref[...] = last_a
    cg_ref[...] = last_g


def _ffn(h2, x2, w_up, w_conv, b_conv, w_down, conv_past, *, tm, fc):
    nb, s, d = x2.shape
    dff = w_down.shape[0]
    nf = dff // fc
    xspec = pl.BlockSpec((None, tm, d), lambda b, i, f: (b, i, 0))
    lo = lambda r: pl.BlockSpec((r, fc), lambda b, i, f: (0, f))
    hi = lambda r: pl.BlockSpec((r, fc), lambda b, i, f: (0, nf + f))
    st_lo = pl.BlockSpec((None, CONV_W - 1, fc), lambda b, i, f: (b, 0, f))
    st_hi = pl.BlockSpec((None, CONV_W - 1, fc), lambda b, i, f: (b, 0, nf + f))
    st_out = pl.BlockSpec((None, None, CONV_W - 1, fc), lambda b, i, f: (b, i, 0, f))
    st_shape = jax.ShapeDtypeStruct((nb, s // tm, CONV_W - 1, dff), F32)
    x3, ca, cg = pl.pallas_call(
        functools.partial(_ffn_kernel, tm=tm, fc=fc),
        out_shape=(jax.ShapeDtypeStruct((nb, s, d), F32), st_shape, st_shape),
        grid=(nb, s // tm, nf),
        in_specs=[xspec, xspec, lo(d), hi(d), lo(CONV_W), hi(CONV_W), lo(1), hi(1),
                  pl.BlockSpec((fc, d), lambda b, i, f: (f, 0)), st_lo, st_hi],
        out_specs=(xspec, st_out, st_out),
        scratch_shapes=[pltpu.VMEM((nf, 4, fc), F32)],
        compiler_params=_cparams(("parallel", "arbitrary", "arbitrary")), name="conv_ffn",
    )(h2, x2, w_up, w_up, w_conv, w_conv, b_conv, b_conv, w_down, conv_past, conv_past)
    return x3, jnp.concatenate([ca[:, -1], cg[:, -1]], axis=-1)


def _final_norm_kernel(x_ref, g_ref, o_ref):
    o_ref[...] = _rms_rows(x_ref[...], g_ref[...])


def _final_norm(x, g, *, tm):
    nb, s, d = x.shape
    spec = pl.BlockSpec((None, tm, d), lambda b, i: (b, i, 0))
    return pl.pallas_call(
        _final_norm_kernel, out_shape=jax.ShapeDtypeStruct(x.shape, F32), grid=(nb, s // tm),
        in_specs=[spec, pl.BlockSpec((1, d), lambda b, i: (0, 0))], out_specs=spec,
        compiler_params=_cparams(("parallel", "parallel")), name="final_norm",
    )(x, g)


def _rope_tables(pos):
    half = A_ROPE // 2
    inv = ROPE_THETA ** (-jnp.arange(half, dtype=F32) / half)
    ang = pos.astype(F32)[:, None] * inv[None, :]
    cos, sin = jnp.cos(ang), jnp.sin(ang)
    z = jnp.zeros((pos.shape[0], 128 - A_ROPE), F32)
    return jnp.concatenate([cos, cos, z], axis=1), jnp.concatenate([-sin, sin, z], axis=1)


def _swap_halves_cols(w):
    half = w.shape[1] // 2
    return jnp.concatenate([w[:, half:], w[:, :half]], axis=1)


def _prep_layer(g_attn, w_in, g_q, w_uq, g_kv, w_uk, w_uv, rel_bias, g_heads, w_out, g_ffn, w_up, w_conv, b_conv,
                w_down):
    d = w_in.shape[0]
    kr = w_in[:, 1024:1024 + A_ROPE]
    z64 = jnp.zeros((d, 64), F32)
    w_ext = jnp.concatenate([w_in[:, :1024], kr, z64, _swap_halves_cols(kr), z64, w_in[:, 1024 + A_ROPE:]], axis=1)
    zq = jnp.zeros((w_uq.shape[0], 64), F32)
    main, swapped = [], []
    for hd in range(A_HEADS):
        lo = hd * (A_NOPE + A_ROPE)
        rope_cols = w_uq[:, lo + A_NOPE:lo + A_NOPE + A_ROPE]
        main += [w_uq[:, lo:lo + A_NOPE], rope_cols, zq]
        swapped += [_swap_halves_cols(rope_cols), zq]
    w_uq_ext = jnp.concatenate(main + swapped, axis=1)
    g_col = g_heads.reshape(-1, 1)
    return dict(
        g_attn=g_attn.reshape(1, -1), g_q=g_q.reshape(1, -1), g_kv=g_kv.reshape(1, -1),
        w_ext=w_ext.astype(BF16), w_uq_ext=w_uq_ext.astype(BF16), w_uk=w_uk.astype(BF16), w_uv=w_uv.astype(BF16),
        bias=_band_bias_table(rel_bias), g_a=g_col[0:512], g_b=g_col[512:1024], g_c=g_col[1024:1536],
        w_out=w_out.astype(BF16), g_ffn=g_ffn.reshape(1, -1), w_up=w_up.astype(BF16), w_conv=w_conv,
        b_conv=b_conv.reshape(1, -1), w_down=w_down.astype(BF16))


def _tile(s, pref):
    t = min(s, pref)
    assert s % t == 0, (s, t)
    return t


def _layer_prompt(x, p, cos_t, sin_t):
    nb, s, d = x.shape
    tm = _tile(s, 256)
    tkc = tm
    (hq, ckv, ckvb, kr, krp, qb, kb, kbb, vb, vbt, qc, kc, kcp, vc, vctp) = _in_proj(
        x, cos_t, sin_t, p["g_attn"], p["g_q"], p["g_kv"], p["w_ext"], tm=tm, tkc=tkc)
    q_a = _q_proj_a(hq, cos_t, sin_t, p["w_uq_ext"], tm=_tile(s, 512))
    k_a, vt_a = _kv_proj_a(ckvb, krp, p["w_uk"], p["w_uv"], tm=_tile(s, 512))
    ta = _tile(s, 512)
    mix_a = _attn_a(q_a, k_a, vt_a, p["g_a"], tq=ta, tk=ta, n_past=0)
    k_pad = jnp.pad(kbb, ((0, 0), (B_PAST, 0), (0, 0)))
    vt_pad = jnp.pad(vbt, ((0, 0), (0, 0), (B_PAST, 0)))
    mix_b = _attn_b(qb, k_pad, vt_pad, p["bias"], p["g_b"], hist0=0)
    mix_c = _attn_c(qc, kcp, vctp, p["g_c"], tq=_tile(s, 512), tk=tkc, n_past=0)
    x2, h2 = _out_proj(mix_a, mix_b, mix_c, x, p["w_out"], p["g_ffn"], tm=_tile(s, 512))
    conv0 = jnp.zeros((nb, CONV_W - 1, p["w_up"].shape[1]), F32)
    x3, conv_new = _ffn(h2, x2, p["w_up"], p["w_conv"], p["b_conv"], p["w_down"], conv0, tm=_tile(s, 512), fc=512)
    rows = min(B_PAST, s)
    state = (ckv, kr, kb[:, s - rows:].reshape(nb, rows, B_HEADS, B_DIM), vb[:, s - rows:].reshape(nb, rows, B_HEADS, B_DIM),
             kc.reshape(nb, s, C_HEADS, C_DIM), vc.reshape(nb, s, C_HEADS, C_DIM), conv_new)
    return x3, state


def _pad_rows(t, n):
    return jnp.pad(t, ((0, 0), (0, n - t.shape[1]), (0, 0)))


def _layer_sample(x, p, cos_t, sin_t, ckv_past, kr_past, bk_past, bv_past, ck_past, cv_past, conv_past):
    nb, s, d = x.shape
    t = nb * s
    n_past = ckv_past.shape[1]
    xf = x.reshape(1, t, d)
    tm = _tile(t, 256)
    (hq, ckv, _, kr, krp, qb, kb, _, vb, _, qc, kc, _, vc, _) = _in_proj(
        xf, cos_t, sin_t, p["g_attn"], p["g_q"], p["g_kv"], p["w_ext"], tm=tm, tkc=tm)
    q_a = _q_proj_a(hq, cos_t, sin_t, p["w_uq_ext"], tm=tm)
    per = lambda a: a.reshape(nb, s, a.shape[-1])
    ckv, kr, kb, vb, kc, vc = per(ckv), per(kr), per(kb), per(vb), per(kc), per(vc)
    tq = 128
    sk = n_past + tq
    ckv_all = _pad_rows(jnp.concatenate([ckv_past, ckv], axis=1), sk).astype(BF16)
    kr_all = jnp.concatenate([kr_past, kr], axis=1)
    krp_all = _pad_rows(jnp.pad(kr_all, ((0, 0), (0, 0), (0, 128 - A_ROPE))), sk).astype(BF16)
    k_a, vt_a = _kv_proj_a(ckv_all, krp_all, p["w_uk"], p["w_uv"], tm=128)
    mix_a = _attn_a(_pad_rows(per(q_a), tq), k_a, vt_a, p["g_a"], tq=tq, tk=128, n_past=n_past)
    kb_all = _pad_rows(jnp.concatenate([bk_past.reshape(nb, -1, 512), kb], axis=1), B_WIN).astype(BF16)
    vb_all = _pad_rows(jnp.concatenate([bv_past.reshape(nb, -1, 512), vb], axis=1), B_WIN).astype(BF16)
    mix_b = _attn_b(_pad_rows(per(qb), B_QTILE), kb_all, vb_all.transpose(0, 2, 1), p["bias"], p["g_b"],
                    hist0=bk_past.shape[1])
    kc_all = _pad_rows(jnp.concatenate([ck_past.reshape(nb, -1, 512), kc], axis=1), sk)
    vc_all = _pad_rows(jnp.concatenate([cv_past.reshape(nb, -1, 512), vc], axis=1), sk)
    kcp = _permute_keys(kc_all, 128).astype(BF16)
    vctp = _permute_keys(vc_all, 128).astype(BF16).transpose(0, 2, 1)
    mix_c = _attn_c(_pad_rows(per(qc), tq), kcp, vctp, p["g_c"], tq=tq, tk=128, n_past=n_past)
    flat = lambda mx: mx[:, :, :s].transpose(1, 0, 2).reshape(1, 512, t)
    x2, h2 = _out_proj(flat(mix_a), flat(mix_b), flat(mix_c), xf, p["w_out"], p["g_ffn"], tm=tm)
    x3, conv_new = _ffn(h2.reshape(nb, s, d), x2.reshape(nb, s, d), p["w_up"], p["w_conv"], p["b_conv"], p["w_down"],
                        conv_past, tm=s, fc=512)
    state = (ckv, kr, kb.reshape(nb, s, B_HEADS, B_DIM), vb.reshape(nb, s, B_HEADS, B_DIM),
             kc.reshape(nb, s, C_HEADS, C_DIM), vc.reshape(nb, s, C_HEADS, C_DIM), conv_new)
    return x3, state


def kernel(x_prompt, x_sample, cache_a_ckv, cache_a_krope, cache_b_k, cache_b_v, cache_c_k, cache_c_v, state_conv,
           g_attn, w_in, g_q, w_uq, g_kv, w_uk, w_uv, rel_bias, g_heads, w_out, g_ffn, w_up, w_conv, b_conv, w_down,
           g_final):
    depth = w_in.shape[0]
    n_tp = x_prompt.shape[1]
    nbs, n_ts, _ = x_sample.shape
    past_len = cache_c_k.shape[2]
    cos_p, sin_p = _rope_tables(jnp.arange(n_tp, dtype=jnp.int32))
    pos_s = jnp.tile(past_len + jnp.arange(n_ts, dtype=jnp.int32), nbs)
    cos_s, sin_s = _rope_tables(pos_s)

    xp, xs = x_prompt, x_sample
    p_layers, s_layers = [], []
    for l in range(depth):
        p = _prep_layer(g_attn[l], w_in[l], g_q[l], w_uq[l], g_kv[l], w_uk[l], w_uv[l], rel_bias[l], g_heads[l],
                        w_out[l], g_ffn[l], w_up[l], w_conv[l], b_conv[l], w_down[l])
        xp, p_st = _layer_prompt(xp, p, cos_p, sin_p)
        xs, s_st = _layer_sample(xs, p, cos_s, sin_s, cache_a_ckv[l], cache_a_krope[l], cache_b_k[l], cache_b_v[l],
                                 cache_c_k[l], cache_c_v[l], state_conv[l])
        p_layers.append(p_st)
        s_layers.append(s_st)

    gf = g_final.reshape(1, -1)
    y_prompt = _final_norm(xp, gf, tm=_tile(n_tp, 512))
    y_sample = _final_norm(xs, gf, tm=n_ts)
    p_out = [jnp.stack(t) for t in zip(*p_layers)]
    s_out = [jnp.stack(t) for t in zip(*s_layers)]
    return (y_prompt, y_sample, *p_out, *s_out)
```

```python
import functools
import math

import numpy as np
import jax
import jax.numpy as jnp
from jax import lax
from jax.experimental import pallas as pl
from jax.experimental.pallas import tpu as pltpu

F32 = jnp.float32
BF16 = jnp.bfloat16

EPS = 1e-6
CHUNK = 64
ROPE_THETA = 10000.0
A_HEADS, A_NOPE, A_ROPE, A_VDIM = 4, 128, 64, 128
A_QK_PAD = 256
B_HEADS, B_DIM, B_PREV_CHUNKS, B_REL_CLIP = 4, 128, 8, 128
B_PAST = B_PREV_CHUNKS * CHUNK
C_HEADS, C_DIM = 4, 128
HEAD_DIM = 128
CONV_W = 3
LOG2E = 1.4426950408889634
A_SCALE = 1.0 / math.sqrt(A_NOPE + A_ROPE)
B_SCALE = 1.0 / math.sqrt(B_DIM)
C_SCALE = 1.0 / math.sqrt(C_DIM)

SB_EXIT_LOG2 = 160.0

B_QTILE = 4 * CHUNK
B_WIN = (B_PREV_CHUNKS + 4) * CHUNK

VMEM_LIMIT = 56 * 1024 * 1024


def _cparams(sem):
    return pltpu.CompilerParams(dimension_semantics=sem, vmem_limit_bytes=VMEM_LIMIT)


def _rms_rows(x, g):
    ms = jnp.mean(x * x, axis=-1, keepdims=True)
    return x * lax.rsqrt(ms + EPS) * g


def _rms_cols(o, g):
    ms = jnp.mean(o * o, axis=0, keepdims=True)
    return o * lax.rsqrt(ms + EPS) * g


_NT = (((1,), (1,)), ((), ()))
_TN = (((0,), (0,)), ((), ()))


def _resident(shape, index_map):
    return pl.BlockSpec(shape, index_map, pipeline_mode=pl.Buffered(1))


IN_COLS = 512 + 512 + 256 + 1536 + 1536


def _in_proj_kernel(x_ref, cos_ref, sin_ref, gattn_ref, gq_ref, gkv_ref, w_ref,
                    hq_ref, ckv_ref, ckvb_ref, kr_ref, krp_ref,
                    qb_ref, kb_ref, kbb_ref, vb_ref, vbt_ref,
                    qc_ref, kc_ref, kcp_ref, vc_ref, vctp_ref, ks_ref, vs_ref, *, tkc):
    h = _rms_rows(x_ref[...], gattn_ref[...]).astype(BF16)

    def proj(lo, n):
        return jnp.dot(h, w_ref[:, lo:lo + n], preferred_element_type=F32)

    hq_ref[...] = _rms_rows(proj(0, 512), gq_ref[...]).astype(BF16)
    ckv = _rms_rows(proj(512, 512), gkv_ref[...])
    ckv_ref[...] = ckv
    ckvb_ref[...] = ckv.astype(BF16)
    kr2 = proj(1024, 256)
    krp = kr2[:, :128] * cos_ref[...] + kr2[:, 128:] * sin_ref[...]
    krp_ref[...] = krp.astype(BF16)
    kr_ref[...] = krp[:, :A_ROPE]
    o = 1280
    qb_ref[...] = (proj(o, 512) * (B_SCALE * LOG2E)).astype(BF16)
    kb = proj(o + 512, 512)
    kb_ref[...] = kb
    kbb_ref[...] = kb.astype(BF16)
    vb = proj(o + 1024, 512)
    vb_ref[...] = vb
    vbt_ref[...] = vb.T.astype(BF16)
    o = 2816
    qc_ref[...] = (proj(o, 512) * (C_SCALE * LOG2E)).astype(BF16)
    kc = proj(o + 512, 512)
    vc = proj(o + 1024, 512)
    kc_ref[...] = kc
    vc_ref[...] = vc
    tm = kc_ref.shape[0]
    m = tkc // 8
    for hd in range(C_HEADS):
        cols = slice(hd * C_DIM, (hd + 1) * C_DIM)
        ks_ref[hd] = kc[:, cols]
        vs_ref[hd] = vc[:, cols]
        for t0 in range(0, tm, tkc):
            kp = jnp.concatenate([ks_ref[hd, pl.ds(t0 + i, 8, stride=m), :] for i in range(m)], axis=0)
            kcp_ref[t0:t0 + tkc, cols] = kp.astype(BF16)
            vp = jnp.concatenate([vs_ref[hd, pl.ds(t0 + i, 8, stride=m), :] for i in range(m)], axis=0)
            vctp_ref[cols, t0:t0 + tkc] = vp.T.astype(BF16)


def _in_proj(x, cos_t, sin_t, g_attn, g_q, g_kv, w_ext, *, tm, tkc):
    nb, s, d = x.shape
    grid = (nb, s // tm)
    tok = lambda c, dt: jax.ShapeDtypeStruct((nb, s, c), dt)
    tr = lambda c: jax.ShapeDtypeStruct((nb, c, s), BF16)
    out_shape = (tok(512, BF16), tok(512, F32), tok(512, BF16), tok(A_ROPE, F32), tok(128, BF16),
                 tok(512, BF16), tok(512, F32), tok(512, BF16), tok(512, F32), tr(512),
                 tok(512, BF16), tok(512, F32), tok(512, BF16), tok(512, F32), tr(512))
    tspec = lambda c: pl.BlockSpec((None, tm, c), lambda b, i: (b, i, 0))
    trspec = pl.BlockSpec((None, 512, tm), lambda b, i: (b, 0, i))
    out_specs = (tspec(512), tspec(512), tspec(512), tspec(A_ROPE), tspec(128),
                 tspec(512), tspec(512), tspec(512), tspec(512), trspec,
                 tspec(512), tspec(512), tspec(512), tspec(512), trspec)
    vec = lambda n: _resident((1, n), lambda b, i: (0, 0))
    in_specs = [tspec(d),
                pl.BlockSpec((tm, 128), lambda b, i: (i, 0)),
                pl.BlockSpec((tm, 128), lambda b, i: (i, 0)),
                vec(d), vec(512), vec(512),
                _resident((d, IN_COLS), lambda b, i: (0, 0))]
    return pl.pallas_call(
        functools.partial(_in_proj_kernel, tkc=tkc),
        out_shape=out_shape, grid=grid, in_specs=in_specs, out_specs=out_specs,
        scratch_shapes=[pltpu.VMEM((C_HEADS, tm, C_DIM), F32), pltpu.VMEM((C_HEADS, tm, C_DIM), F32)],
        compiler_params=_cparams(("parallel", "parallel")), name="in_proj",
    )(x, cos_t, sin_t, g_attn, g_q, g_kv, w_ext)


def _q_proj_a_kernel(hq_ref, cos_ref, sin_ref, w_ref, q_ref):
    ql = jnp.dot(hq_ref[...], w_ref[...], preferred_element_type=F32)
    cos, sin = cos_ref[...], sin_ref[...]
    sc = A_SCALE * LOG2E
    for hd in range(A_HEADS):
        lo = hd * A_QK_PAD
        rot = ql[:, lo + 128:lo + 256] * cos + ql[:, 1024 + hd * 128:1024 + (hd + 1) * 128] * sin
        q_ref[:, lo:lo + 128] = (ql[:, lo:lo + 128] * sc).astype(BF16)
        q_ref[:, lo + 128:lo + 256] = (rot * sc).astype(BF16)


def _q_proj_a(hq, cos_t, sin_t, w_uq_ext, *, tm):
    nb, s, _ = hq.shape
    return pl.pallas_call(
        _q_proj_a_kernel,
        out_shape=jax.ShapeDtypeStruct((nb, s, A_HEADS * A_QK_PAD), BF16),
        grid=(nb, s // tm),
        in_specs=[pl.BlockSpec((None, tm, 512), lambda b, i: (b, i, 0)),
                  pl.BlockSpec((tm, 128), lambda b, i: (i, 0)),
                  pl.BlockSpec((tm, 128), lambda b, i: (i, 0)),
                  _resident((512, 1536), lambda b, i: (0, 0))],
        out_specs=pl.BlockSpec((None, tm, A_HEADS * A_QK_PAD), lambda b, i: (b, i, 0)),
        compiler_params=_cparams(("parallel", "parallel")), name="q_proj_a",
    )(hq, cos_t, sin_t, w_uq_ext)


def _kv_proj_a_kernel(ckv_ref, krp_ref, wk_ref, wv_ref, k_ref, vt_ref):
    c = ckv_ref[...]
    kn = jnp.dot(c, wk_ref[...], preferred_element_type=F32)
    v = jnp.dot(c, wv_ref[...], preferred_element_type=F32)
    krp = krp_ref[...]
    for hd in range(A_HEADS):
        lo = hd * A_QK_PAD
        k_ref[:, lo:lo + 128] = kn[:, hd * 128:(hd + 1) * 128].astype(BF16)
        k_ref[:, lo + 128:lo + 256] = krp
    vt_ref[...] = v.T.astype(BF16)


def _kv_proj_a(ckvb, krp, w_uk, w_uv, *, tm):
    nb, s, _ = ckvb.shape
    return pl.pallas_call(
        _kv_proj_a_kernel,
        out_shape=(jax.ShapeDtypeStruct((nb, s, A_HEADS * A_QK_PAD), BF16),
                   jax.ShapeDtypeStruct((nb, A_HEADS * A_VDIM, s), BF16)),
        grid=(nb, s // tm),
        in_specs=[pl.BlockSpec((None, tm, 512), lambda b, i: (b, i, 0)),
                  pl.BlockSpec((None, tm, 128), lambda b, i: (b, i, 0)),
                  _resident((512, 512), lambda b, i: (0, 0)),
                  _resident((512, 512), lambda b, i: (0, 0))],
        out_specs=(pl.BlockSpec((None, tm, A_HEADS * A_QK_PAD), lambda b, i: (b, i, 0)),
                   pl.BlockSpec((None, A_HEADS * A_VDIM, tm), lambda b, i: (b, 0, i))),
        compiler_params=_cparams(("parallel", "parallel")), name="kv_proj_a",
    )(ckvb, krp, w_uk, w_uv)


def _attn_a_kernel(q_ref, k_ref, vt_ref, g_ref, o_ref, s0_ref, s1_ref, m_ref, l_ref, acc_ref, *, tq, tk, n_past, nh):
    qi = pl.program_id(2)
    q0 = n_past + qi * tq
    n_full = q0 // tk
    m_ref[...] = jnp.full(m_ref.shape, -jnp.inf, F32)
    l_ref[...] = jnp.zeros(l_ref.shape, F32)
    acc_ref[...] = jnp.zeros(acc_ref.shape, F32)

    def scores(j, s_ref):
        start = pl.multiple_of(j * tk, tk)
        for hd in range(nh):
            qk = slice(hd * A_QK_PAD, (hd + 1) * A_QK_PAD)
            s_ref[hd] = lax.dot_general(k_ref[pl.ds(start, tk), qk], q_ref[:, qk], _NT, preferred_element_type=F32)

    def softmax_pv(j, s_ref, masked):
        start = pl.multiple_of(j * tk, tk)
        if masked:
            kpos = start + lax.broadcasted_iota(jnp.int32, (tk, tq), 0)
            qpos = q0 + lax.broadcasted_iota(jnp.int32, (tk, tq), 1)
            visible = (kpos >> 6) <= (qpos >> 6)
        for hd in range(nh):
            vv = slice(hd * A_VDIM, (hd + 1) * A_VDIM)
            s = s_ref[hd]
            if masked:
                s = jnp.where(visible, s, -jnp.inf)
            m_old = m_ref[hd]
            m_new = jnp.maximum(m_old, jnp.max(s, axis=0, keepdims=True))
            p = jnp.exp2(s - m_new)
            alpha = jnp.exp2(m_old - m_new)
            l_ref[hd] = alpha * l_ref[hd] + jnp.sum(p, axis=0, keepdims=True)
            pv = jnp.dot(vt_ref[vv, pl.ds(start, tk)], p.astype(BF16), preferred_element_type=F32)
            acc_ref[vv, :] = alpha * acc_ref[vv, :] + pv
            m_ref[hd] = m_new

    def pair_body(t, c):
        j = 2 * t
        scores(j + 1, s1_ref)
        softmax_pv(j, s0_ref, False)
        scores(j + 2, s0_ref)
        softmax_pv(j + 1, s1_ref, False)
        return c

    scores(0, s0_ref)
    lax.fori_loop(0, n_full // 2, pair_body, 0)
    scores(n_full + 1, s1_ref)
    softmax_pv(n_full, s0_ref, True)
    softmax_pv(n_full + 1, s1_ref, True)
    for hd in range(nh):
        vv = slice(hd * A_VDIM, (hd + 1) * A_VDIM)
        o = acc_ref[vv, :] / l_ref[hd]
        o_ref[vv, :] = _rms_cols(o, g_ref[vv, :]).astype(BF16)


def _attn_a(q, k, vt, g_col, *, tq, tk, n_past, nh):
    nb, sq, _ = q.shape
    sk = k.shape[1]
    assert tq == 2 * tk and n_past % tq == 0 and sq % tq == 0 and sk >= n_past + sq
    return pl.pallas_call(
        functools.partial(_attn_a_kernel, tq=tq, tk=tk, n_past=n_past, nh=nh),
        out_shape=jax.ShapeDtypeStruct((nb, A_HEADS * A_VDIM, sq), BF16),
        grid=(nb, A_HEADS // nh, sq // tq),
        in_specs=[pl.BlockSpec((None, tq, nh * A_QK_PAD), lambda b, h, i: (b, i, h)),
                  _resident((None, sk, nh * A_QK_PAD), lambda b, h, i: (b, 0, h)),
                  _resident((None, nh * A_VDIM, sk), lambda b, h, i: (b, h, 0)),
                  pl.BlockSpec((nh * A_VDIM, 1), lambda b, h, i: (h, 0))],
        out_specs=pl.BlockSpec((None, nh * A_VDIM, tq), lambda b, h, i: (b, h, i)),
        scratch_shapes=[pltpu.VMEM((nh, tk, tq), F32), pltpu.VMEM((nh, tk, tq), F32),
                        pltpu.VMEM((nh, 1, tq), F32), pltpu.VMEM((nh, 1, tq), F32),
                        pltpu.VMEM((nh * A_VDIM, tq), F32)],
        compiler_params=_cparams(("parallel", "parallel", "arbitrary")), name="attn_a",
    )(q, k, vt, g_col)


def _attn_b_kernel(q_ref, k0_ref, k1_ref, k2_ref, v0_ref, v1_ref, v2_ref, bias_ref, g_ref, o_ref, *, hist0):
    st = pl.program_id(1)
    thr = B_PAST - hist0 - st * B_QTILE
    row_ok = lax.broadcasted_iota(jnp.int32, (B_WIN, B_QTILE), 0) >= thr
    for hd in range(B_HEADS):
        cols = slice(hd * B_DIM, (hd + 1) * B_DIM)
        k = jnp.concatenate([k0_ref[:, cols], k1_ref[:, cols], k2_ref[:, cols]], axis=0)
        s = lax.dot_general(k, q_ref[:, cols], _NT, preferred_element_type=F32) + bias_ref[hd]
        s = jnp.where(row_ok, s, -jnp.inf)
        m = jnp.max(s, axis=0, keepdims=True)
        p = jnp.exp2(s - m)
        l = jnp.sum(p, axis=0, keepdims=True)
        v = jnp.concatenate([v0_ref[cols, :], v1_ref[cols, :], v2_ref[cols, :]], axis=1)
        o = jnp.dot(v, p.astype(BF16), preferred_element_type=F32) / l
        o_ref[cols, :] = _rms_cols(o, g_ref[cols, :]).astype(BF16)


def _attn_b(q, k_pad, vt_pad, bias, g_col, *, hist0):
    nb, sq, w = q.shape
    kspec = lambda o: pl.BlockSpec((None, B_QTILE, w), lambda b, i: (b, i + o, 0))
    vspec = lambda o: pl.BlockSpec((None, w, B_QTILE), lambda b, i: (b, 0, i + o))
    return pl.pallas_call(
        functools.partial(_attn_b_kernel, hist0=hist0),
        out_shape=jax.ShapeDtypeStruct((nb, w, sq), BF16),
        grid=(nb, sq // B_QTILE),
        in_specs=[pl.BlockSpec((None, B_QTILE, w), lambda b, i: (b, i, 0)),
                  kspec(0), kspec(1), kspec(2), vspec(0), vspec(1), vspec(2),
                  pl.BlockSpec((B_HEADS, B_WIN, B_QTILE), lambda b, i: (0, 0, 0)),
                  pl.BlockSpec((w, 1), lambda b, i: (0, 0))],
        out_specs=pl.BlockSpec((None, w, B_QTILE), lambda b, i: (b, 0, i)),
        compiler_params=_cparams(("parallel", "arbitrary")), name="attn_b",
    )(q, k_pad, k_pad, k_pad, vt_pad, vt_pad, vt_pad, bias, g_col)


def _band_bias_table(rel_bias):
    nh = rel_bias.shape[0]
    ring = 1024
    n_lo = B_WIN - 1 - (B_PAST + B_REL_CLIP)
    n_hi = B_PAST - B_REL_CLIP - 1
    lo = jnp.broadcast_to(rel_bias[:, :1], (nh, n_lo))
    hi = lambda n: jnp.broadcast_to(rel_bias[:, -1:], (nh, n))
    gap = ring - (B_QTILE + n_lo + rel_bias.shape[1] + n_hi)
    v = jnp.concatenate([hi(B_QTILE), jnp.zeros((nh, gap), rel_bias.dtype), lo, rel_bias, hi(n_hi)], axis=1)
    tbl = jnp.tile(v, (1, B_WIN))[:, :B_WIN * (ring - 1)].reshape(nh, B_WIN, ring - 1)[:, :, :B_QTILE] * LOG2E
    kc = np.arange(B_WIN)[:, None] // CHUNK
    qc = np.arange(B_QTILE)[None, :] // CHUNK
    valid = (kc >= qc) & (kc <= qc + B_PREV_CHUNKS)
    return jnp.where(valid[None], tbl, -jnp.inf)


def _attn_c_kernel(q_ref, kp_ref, vtp_ref, g_ref, o_ref, c_ref, acc_ref, *, tq, tk, n_past):
    qi = pl.program_id(2)
    q = q_ref[...]
    q0 = n_past + qi * tq
    n_full = q0 // tk
    n_mask = tq // tk
    m = tk // 8
    c_ref[...] = jnp.zeros(c_ref.shape, F32)
    acc_ref[...] = jnp.zeros(acc_ref.shape, F32)
    sub = lax.broadcasted_iota(jnp.int32, (8, tq), 0)

    def tile(j, masked):
        start = pl.multiple_of(j * tk, tk)
        z = lax.dot_general(kp_ref[pl.ds(start, tk), :], q, _NT, preferred_element_type=F32)
        sp = jnp.maximum(z, 0.0) + jnp.log2(1.0 + jnp.exp2(-jnp.abs(z)))
        if masked:
            r = lax.broadcasted_iota(jnp.int32, (tk, tq), 0)
            kpos = start + (r & 7) * m + (r >> 3)
            qpos = q0 + lax.broadcasted_iota(jnp.int32, (tk, tq), 1)
            causal = kpos < qpos
            sp = jnp.where(causal, sp, 0.0)
        run = [None] * m
        run[m - 1] = sp[(m - 1) * 8:m * 8]
        for i in range(m - 2, -1, -1):
            run[i] = sp[i * 8:(i + 1) * 8] + run[i + 1]
        tot = run[0]
        incl = tot
        for sh in (1, 2, 4):
            incl = incl + jnp.where(sub + sh < 8, pltpu.roll(incl, 8 - sh, 0), 0.0)
        c_in = c_ref[...]
        off = (incl - tot) + c_in
        rows = []
        for i in range(m):
            w = jnp.exp2(z[i * 8:(i + 1) * 8] - (run[i] + off))
            if masked:
                w = jnp.where(causal[i * 8:(i + 1) * 8], w, 0.0)
            rows.append(w)
        w = jnp.concatenate(rows, axis=0).astype(BF16)
        acc_ref[...] += jnp.dot(vtp_ref[:, pl.ds(start, tk)], w, preferred_element_type=F32)
        c_new = c_in + incl[0:1, :]
        c_ref[...] = c_new
        return jnp.min(c_new)

    cmin = jnp.float32(0.0)
    for jj in range(n_mask - 1, -1, -1):
        cmin = tile(n_full + jj, True)

    def cond(carry):
        j, cm = carry
        return jnp.logical_and(j >= 0, cm < SB_EXIT_LOG2)

    def body(carry):
        j, _ = carry
        return j - 1, tile(j, False)

    lax.while_loop(cond, body, (n_full - 1, cmin))
    o_ref[...] = _rms_cols(acc_ref[...], g_ref[...]).astype(BF16)


def _attn_c(q, kp, vtp, g_col, *, tq, tk, n_past):
    nb, sq, _ = q.shape
    sk = kp.shape[1]
    return pl.pallas_call(
        functools.partial(_attn_c_kernel, tq=tq, tk=tk, n_past=n_past),
        out_shape=jax.ShapeDtypeStruct((nb, C_HEADS * C_DIM, sq), BF16),
        grid=(nb, C_HEADS, sq // tq),
        in_specs=[pl.BlockSpec((None, tq, C_DIM), lambda b, h, i: (b, i, h)),
                  pl.BlockSpec((None, sk, C_DIM), lambda b, h, i: (b, 0, h)),
                  pl.BlockSpec((None, C_DIM, sk), lambda b, h, i: (b, h, 0)),
                  pl.BlockSpec((C_DIM, 1), lambda b, h, i: (h, 0))],
        out_specs=pl.BlockSpec((None, C_DIM, tq), lambda b, h, i: (b, h, i)),
        scratch_shapes=[pltpu.VMEM((1, tq), F32), pltpu.VMEM((C_DIM, tq), F32)],
        compiler_params=_cparams(("parallel", "parallel", "arbitrary")), name="attn_c",
    )(q, kp, vtp, g_col)


def _permute_keys(t, tk):
    nb, s, c = t.shape
    m = tk // 8
    return t.reshape(nb, s // tk, 8, m, c).transpose(0, 1, 3, 2, 4).reshape(nb, s, c)


def _out_proj_kernel(ma_ref, mb_ref, mc_ref, x_ref, w_ref, g_ref, x2_ref, h2_ref):
    y = lax.dot_general(ma_ref[...], w_ref[0:512, :], _TN, preferred_element_type=F32)
    y = y + lax.dot_general(mb_ref[...], w_ref[512:1024, :], _TN, preferred_element_type=F32)
    y = y + lax.dot_general(mc_ref[...], w_ref[1024:1536, :], _TN, preferred_element_type=F32)
    x2 = x_ref[...] + y
    x2_ref[...] = x2
    h2_ref[...] = _rms_rows(x2, g_ref[...]).astype(BF16)


def _out_proj(ma, mb, mc, x, w_out, g_ffn, *, tm):
    nb, s, d = x.shape
    mspec = pl.BlockSpec((None, 512, tm), lambda b, i: (b, 0, i))
    xspec = pl.BlockSpec((None, tm, d), lambda b, i: (b, i, 0))
    return pl.pallas_call(
        _out_proj_kernel,
        out_shape=(jax.ShapeDtypeStruct((nb, s, d), F32), jax.ShapeDtypeStruct((nb, s, d), BF16)),
        grid=(nb, s // tm),
        in_specs=[mspec, mspec, mspec, xspec,
                  _resident((1536, d), lambda b, i: (0, 0)),
                  _resident((1, d), lambda b, i: (0, 0))],
        out_specs=(xspec, xspec),
        compiler_params=_cparams(("parallel", "parallel")), name="out_proj",
    )(ma, mb, mc, x, w_out, g_ffn)


def _ffn_kernel(h_ref, x_ref, wa_ref, wg_ref, wca_ref, wcg_ref, ba_ref, bg_ref, wd_ref, pa_ref, pg_ref,
                o_ref, ca_ref, cg_ref, carry_ref, act_ref, *, tm, fc, nf):
    i = pl.program_id(1)
    f = pl.program_id(2)

    def up(fi):
        slot = lax.rem(fi, 2)
        h = h_ref[...]
        row = lax.broadcasted_iota(jnp.int32, (tm, fc), 0)

        def conv(u, prev, wc_ref, b_ref):
            u1 = jnp.where(row == 0, prev[1:2, :], pltpu.roll(u, 1, 0))
            u2 = jnp.where(row == 0, prev[0:1, :], jnp.where(row == 1, prev[1:2, :], pltpu.roll(u, 2, 0)))
            return b_ref[...] + wc_ref[0:1, :] * u2 + wc_ref[1:2, :] * u1 + wc_ref[2:3, :] * u

        ua = jnp.dot(h, wa_ref[...], preferred_element_type=F32)
        ug = jnp.dot(h, wg_ref[...], preferred_element_type=F32)
        a = conv(ua, carry_ref[fi, 0:2, :], wca_ref, ba_ref)
        g = conv(ug, carry_ref[fi, 2:4, :], wcg_ref, bg_ref)
        act_ref[slot] = (a * (1.0 / (1.0 + jnp.exp(-a))) * g).astype(BF16)
        last_a = ua[tm - 2:tm, :]
        last_g = ug[tm - 2:tm, :]
        carry_ref[fi, 0:2, :] = last_a
        carry_ref[fi, 2:4, :] = last_g
        ca_ref[...] = last_a
        cg_ref[...] = last_g

    def down(fi):
        o_ref[...] += jnp.dot(act_ref[lax.rem(fi, 2)], wd_ref[...], preferred_element_type=F32)

    @pl.when(jnp.logical_and(i == 0, f < nf))
    def _():
        carry_ref[f, 0:2, :] = pa_ref[...]
        carry_ref[f, 2:4, :] = pg_ref[...]

    @pl.when(f == 0)
    def _():
        o_ref[...] = x_ref[...]
        up(f)

    @pl.when(jnp.logical_and(f > 0, f < nf))
    def _():
        down(f - 1)
        up(f)

    @pl.when(f == nf)
    def _():
        down(f - 1)


def _ffn(h2, x2, w_up, w_conv, b_conv, w_down, conv_past, *, tm, fc):
    nb, s, d = x2.shape
    dff = w_down.shape[0]
    nf = dff // fc
    cur = lambda f: jnp.minimum(f, nf - 1)
    xspec = pl.BlockSpec((None, tm, d), lambda b, i, f: (b, i, 0))
    lo = lambda r: pl.BlockSpec((r, fc), lambda b, i, f: (0, cur(f)))
    hi = lambda r: pl.BlockSpec((r, fc), lambda b, i, f: (0, nf + cur(f)))
    st_lo = pl.BlockSpec((None, CONV_W - 1, fc), lambda b, i, f: (b, 0, cur(f)))
    st_hi = pl.BlockSpec((None, CONV_W - 1, fc), lambda b, i, f: (b, 0, nf + cur(f)))
    st_out = pl.BlockSpec((None, None, CONV_W - 1, fc), lambda b, i, f: (b, i, 0, cur(f)))
    st_shape = jax.ShapeDtypeStruct((nb, s // tm, CONV_W - 1, dff), F32)
    x3, ca, cg = pl.pallas_call(
        functools.partial(_ffn_kernel, tm=tm, fc=fc, nf=nf),
        out_shape=(jax.ShapeDtypeStruct((nb, s, d), F32), st_shape, st_shape),
        grid=(nb, s // tm, nf + 1),
        in_specs=[xspec, xspec, lo(d), hi(d), lo(CONV_W), hi(CONV_W), lo(1), hi(1),
                  pl.BlockSpec((fc, d), lambda b, i, f: (jnp.maximum(f - 1, 0), 0)), st_lo, st_hi],
        out_specs=(xspec, st_out, st_out),
        scratch_shapes=[pltpu.VMEM((nf, 4, fc), F32), pltpu.VMEM((2, tm, fc), BF16)],
        compiler_params=_cparams(("parallel", "arbitrary", "arbitrary")), name="conv_ffn",
    )(h2, x2, w_up, w_up, w_conv, w_conv, b_conv, b_conv, w_down, conv_past, conv_past)
    return x3, jnp.concatenate([ca[:, -1], cg[:, -1]], axis=-1)


def _final_norm_kernel(x_ref, g_ref, o_ref):
    o_ref[...] = _rms_rows(x_ref[...], g_ref[...])


def _final_norm(x, g, *, tm):
    nb, s, d = x.shape
    spec = pl.BlockSpec((None, tm, d), lambda b, i: (b, i, 0))
    return pl.pallas_call(
        _final_norm_kernel, out_shape=jax.ShapeDtypeStruct(x.shape, F32), grid=(nb, s // tm),
        in_specs=[spec, pl.BlockSpec((1, d), lambda b, i: (0, 0))], out_specs=spec,
        compiler_params=_cparams(("parallel", "parallel")), name="final_norm",
    )(x, g)


def _rope_tables(pos):
    half = A_ROPE // 2
    inv = ROPE_THETA ** (-jnp.arange(half, dtype=F32) / half)
    ang = pos.astype(F32)[:, None] * inv[None, :]
    cos, sin = jnp.cos(ang), jnp.sin(ang)
    z = jnp.zeros((pos.shape[0], 128 - A_ROPE), F32)
    return jnp.concatenate([cos, cos, z], axis=1), jnp.concatenate([-sin, sin, z], axis=1)


def _swap_halves_cols(w):
    half = w.shape[1] // 2
    return jnp.concatenate([w[:, half:], w[:, :half]], axis=1)


def _prep_layer(g_attn, w_in, g_q, w_uq, g_kv, w_uk, w_uv, rel_bias, g_heads, w_out, g_ffn, w_up, w_conv, b_conv,
                w_down):
    d = w_in.shape[0]
    kr = w_in[:, 1024:1024 + A_ROPE]
    z64 = jnp.zeros((d, 64), F32)
    w_ext = jnp.concatenate([w_in[:, :1024], kr, z64, _swap_halves_cols(kr), z64, w_in[:, 1024 + A_ROPE:]], axis=1)
    zq = jnp.zeros((w_uq.shape[0], 64), F32)
    main, swapped = [], []
    for hd in range(A_HEADS):
        lo = hd * (A_NOPE + A_ROPE)
        rope_cols = w_uq[:, lo + A_NOPE:lo + A_NOPE + A_ROPE]
        main += [w_uq[:, lo:lo + A_NOPE], rope_cols, zq]
        swapped += [_swap_halves_cols(rope_cols), zq]
    w_uq_ext = jnp.concatenate(main + swapped, axis=1)
    g_col = g_heads.reshape(-1, 1)
    return dict(
        g_attn=g_attn.reshape(1, -1), g_q=g_q.reshape(1, -1), g_kv=g_kv.reshape(1, -1),
        w_ext=w_ext.astype(BF16), w_uq_ext=w_uq_ext.astype(BF16), w_uk=w_uk.astype(BF16), w_uv=w_uv.astype(BF16),
        bias=_band_bias_table(rel_bias), g_a=g_col[0:512], g_b=g_col[512:1024], g_c=g_col[1024:1536],
        w_out=w_out.astype(BF16), g_ffn=g_ffn.reshape(1, -1), w_up=w_up.astype(BF16), w_conv=w_conv,
        b_conv=b_conv.reshape(1, -1), w_down=w_down.astype(BF16))


def _tile(s, pref):
    t = min(s, pref)
    assert s % t == 0, (s, t)
    return t


def _layer_prompt(x, p, cos_t, sin_t):
    nb, s, d = x.shape
    tm = _tile(s, 256)
    tkc = tm
    (hq, ckv, ckvb, kr, krp, qb, kb, kbb, vb, vbt, qc, kc, kcp, vc, vctp) = _in_proj(
        x, cos_t, sin_t, p["g_attn"], p["g_q"], p["g_kv"], p["w_ext"], tm=tm, tkc=tkc)
    q_a = _q_proj_a(hq, cos_t, sin_t, p["w_uq_ext"], tm=_tile(s, 512))
    k_a, vt_a = _kv_proj_a(ckvb, krp, p["w_uk"], p["w_uv"], tm=_tile(s, 512))
    ta = _tile(s, 1024)
    mix_a = _attn_a(q_a, k_a, vt_a, p["g_a"], tq=ta, tk=ta // 2, n_past=0, nh=2)
    k_pad = jnp.pad(kbb, ((0, 0), (B_PAST, 0), (0, 0)))
    vt_pad = jnp.pad(vbt, ((0, 0), (0, 0), (B_PAST, 0)))
    mix_b = _attn_b(qb, k_pad, vt_pad, p["bias"], p["g_b"], hist0=0)
    mix_c = _attn_c(qc, kcp, vctp, p["g_c"], tq=_tile(s, 512), tk=tkc, n_past=0)
    x2, h2 = _out_proj(mix_a, mix_b, mix_c, x, p["w_out"], p["g_ffn"], tm=_tile(s, 512))
    conv0 = jnp.zeros((nb, CONV_W - 1, p["w_up"].shape[1]), F32)
    x3, conv_new = _ffn(h2, x2, p["w_up"], p["w_conv"], p["b_conv"], p["w_down"], conv0, tm=_tile(s, 512), fc=512)
    rows = min(B_PAST, s)
    state = (ckv, kr, kb[:, s - rows:].reshape(nb, rows, B_HEADS, B_DIM), vb[:, s - rows:].reshape(nb, rows, B_HEADS, B_DIM),
             kc.reshape(nb, s, C_HEADS, C_DIM), vc.reshape(nb, s, C_HEADS, C_DIM), conv_new)
    return x3, state


def _pad_rows(t, n):
    return jnp.pad(t, ((0, 0), (0, n - t.shape[1]), (0, 0)))


def _layer_sample(x, p, cos_t, sin_t, ckv_past, kr_past, bk_past, bv_past, ck_past, cv_past, conv_past):
    nb, s, d = x.shape
    t = nb * s
    n_past = ckv_past.shape[1]
    xf = x.reshape(1, t, d)
    tm = _tile(t, 256)
    (hq, ckv, _, kr, krp, qb, kb, _, vb, _, qc, kc, _, vc, _) = _in_proj(
        xf, cos_t, sin_t, p["g_attn"], p["g_q"], p["g_kv"], p["w_ext"], tm=tm, tkc=tm)
    q_a = _q_proj_a(hq, cos_t, sin_t, p["w_uq_ext"], tm=tm)
    per = lambda a: a.reshape(nb, s, a.shape[-1])
    ckv, kr, kb, vb, kc, vc = per(ckv), per(kr), per(kb), per(vb), per(kc), per(vc)
    tqa = 256
    ska = n_past + tqa
    ckv_all = _pad_rows(jnp.concatenate([ckv_past, ckv], axis=1), ska).astype(BF16)
    kr_all = jnp.concatenate([kr_past, kr], axis=1)
    krp_all = _pad_rows(jnp.pad(kr_all, ((0, 0), (0, 0), (0, 128 - A_ROPE))), ska).astype(BF16)
    k_a, vt_a = _kv_proj_a(ckv_all, krp_all, p["w_uk"], p["w_uv"], tm=128)
    mix_a = _attn_a(_pad_rows(per(q_a), tqa), k_a, vt_a, p["g_a"], tq=tqa, tk=tqa // 2, n_past=n_past, nh=2)
    tq = 128
    sk = n_past + tq
    kb_all = _pad_rows(jnp.concatenate([bk_past.reshape(nb, -1, 512), kb], axis=1), B_WIN).astype(BF16)
    vb_all = _pad_rows(jnp.concatenate([bv_past.reshape(nb, -1, 512), vb], axis=1), B_WIN).astype(BF16)
    mix_b = _attn_b(_pad_rows(per(qb), B_QTILE), kb_all, vb_all.transpose(0, 2, 1), p["bias"], p["g_b"],
                    hist0=bk_past.shape[1])
    kc_all = _pad_rows(jnp.concatenate([ck_past.reshape(nb, -1, 512), kc], axis=1), sk)
    vc_all = _pad_rows(jnp.concatenate([cv_past.reshape(nb, -1, 512), vc], axis=1), sk)
    kcp = _permute_keys(kc_all, 128).astype(BF16)
    vctp = _permute_keys(vc_all, 128).astype(BF16).transpose(0, 2, 1)
    mix_c = _attn_c(_pad_rows(per(qc), tq), kcp, vctp, p["g_c"], tq=tq, tk=128, n_past=n_past)
    flat = lambda mx: mx[:, :, :s].transpose(1, 0, 2).reshape(1, 512, t)
    x2, h2 = _out_proj(flat(mix_a), flat(mix_b), flat(mix_c), xf, p["w_out"], p["g_ffn"], tm=tm)
    x3, conv_new = _ffn(h2.reshape(nb, s, d), x2.reshape(nb, s, d), p["w_up"], p["w_conv"], p["b_conv"], p["w_down"],
                        conv_past, tm=s, fc=512)
    state = (ckv, kr, kb.reshape(nb, s, B_HEADS, B_DIM), vb.reshape(nb, s, B_HEADS, B_DIM),
             kc.reshape(nb, s, C_HEADS, C_DIM), vc.reshape(nb, s, C_HEADS, C_DIM), conv_new)
    return x3, state


def kernel(x_prompt, x_sample, cache_a_ckv, cache_a_krope, cache_b_k, cache_b_v, cache_c_k, cache_c_v, state_conv,
           g_attn, w_in, g_q, w_uq, g_kv, w_uk, w_uv, rel_bias, g_heads, w_out, g_ffn, w_up, w_conv, b_conv, w_down,
           g_final):
    depth = w_in.shape[0]
    n_tp = x_prompt.shape[1]
    nbs, n_ts, _ = x_sample.shape
    past_len = cache_c_k.shape[2]
    cos_p, sin_p = _rope_tables(jnp.arange(n_tp, dtype=jnp.int32))
    pos_s = jnp.tile(past_len + jnp.arange(n_ts, dtype=jnp.int32), nbs)
    cos_s, sin_s = _rope_tables(pos_s)

    xp, xs = x_prompt, x_sample
    p_layers, s_layers = [], []
    for l in range(depth):
        p = _prep_layer(g_attn[l], w_in[l], g_q[l], w_uq[l], g_kv[l], w_uk[l], w_uv[l], rel_bias[l], g_heads[l],
                        w_out[l], g_ffn[l], w_up[l], w_conv[l], b_conv[l], w_down[l])
        xp, p_st = _layer_prompt(xp, p, cos_p, sin_p)
        xs, s_st = _layer_sample(xs, p, cos_s, sin_s, cache_a_ckv[l], cache_a_krope[l], cache_b_k[l], cache_b_v[l],
                                 cache_c_k[l], cache_c_v[l], state_conv[l])
        p_layers.append(p_st)
        s_layers.append(s_st)

    gf = g_final.reshape(1, -1)
    y_prompt = _final_norm(xp, gf, tm=_tile(n_tp, 512))
    y_sample = _final_norm(xs, gf, tm=n_ts)
    p_out = [jnp.stack(t) for t in zip(*p_layers)]
    s_out = [jnp.stack(t) for t in zip(*s_layers)]
    return (y_prompt, y_sample, *p_out, *s_out)
```

```python
import functools
import math

import numpy as np
import jax
import jax.numpy as jnp
from jax import lax
from jax.experimental import pallas as pl
from jax.experimental.pallas import tpu as pltpu

F32 = jnp.float32
BF16 = jnp.bfloat16

EPS = 1e-6
CHUNK = 64
ROPE_THETA = 10000.0
A_HEADS, A_NOPE, A_ROPE, A_VDIM = 4, 128, 64, 128
A_QK_PAD = 256
B_HEADS, B_DIM, B_PREV_CHUNKS, B_REL_CLIP = 4, 128, 8, 128
B_PAST = B_PREV_CHUNKS * CHUNK
C_HEADS, C_DIM = 4, 128
HEAD_DIM = 128
CONV_W = 3
LOG2E = 1.4426950408889634
A_SCALE = 1.0 / math.sqrt(A_NOPE + A_ROPE)
B_SCALE = 1.0 / math.sqrt(B_DIM)
C_SCALE = 1.0 / math.sqrt(C_DIM)

SB_EXIT_LOG2 = 160.0

B_QTILE = 4 * CHUNK
B_WIN = (B_PREV_CHUNKS + 4) * CHUNK

VMEM_LIMIT = 56 * 1024 * 1024


def _cparams(sem):
    return pltpu.CompilerParams(dimension_semantics=sem, vmem_limit_bytes=VMEM_LIMIT)


def _rms_rows(x, g):
    ms = jnp.mean(x * x, axis=-1, keepdims=True)
    return x * lax.rsqrt(ms + EPS) * g


def _rms_cols(o, g):
    ms = jnp.mean(o * o, axis=0, keepdims=True)
    return o * lax.rsqrt(ms + EPS) * g


_NT = (((1,), (1,)), ((), ()))
_TN = (((0,), (0,)), ((), ()))


def _resident(shape, index_map):
    return pl.BlockSpec(shape, index_map, pipeline_mode=pl.Buffered(1))


IN_COLS = 512 + 512 + 256 + 1536 + 1536


def _in_proj_kernel(x_ref, cos_ref, sin_ref, gattn_ref, gq_ref, gkv_ref, w_ref,
                    hq_ref, ckv_ref, ckvb_ref, kr_ref, krp_ref,
                    qb_ref, kb_ref, kbb_ref, vb_ref, vbt_ref,
                    qc_ref, kc_ref, kcp_ref, vc_ref, vctp_ref, ks_ref, vs_ref, *, tkc):
    h = _rms_rows(x_ref[...], gattn_ref[...]).astype(BF16)

    def proj(lo, n):
        return jnp.dot(h, w_ref[:, lo:lo + n], preferred_element_type=F32)

    hq_ref[...] = _rms_rows(proj(0, 512), gq_ref[...]).astype(BF16)
    ckv = _rms_rows(proj(512, 512), gkv_ref[...])
    ckv_ref[...] = ckv
    ckvb_ref[...] = ckv.astype(BF16)
    kr2 = proj(1024, 256)
    krp = kr2[:, :128] * cos_ref[...] + kr2[:, 128:] * sin_ref[...]
    krp_ref[...] = krp.astype(BF16)
    kr_ref[...] = krp[:, :A_ROPE]
    o = 1280
    qb_ref[...] = (proj(o, 512) * (B_SCALE * LOG2E)).astype(BF16)
    kb = proj(o + 512, 512)
    kb_ref[...] = kb
    kbb_ref[...] = kb.astype(BF16)
    vb = proj(o + 1024, 512)
    vb_ref[...] = vb
    vbt_ref[...] = vb.T.astype(BF16)
    o = 2816
    qc_ref[...] = (proj(o, 512) * (C_SCALE * LOG2E)).astype(BF16)
    kc = proj(o + 512, 512)
    vc = proj(o + 1024, 512)
    kc_ref[...] = kc
    vc_ref[...] = vc
    tm = kc_ref.shape[0]
    m = tkc // 8
    for hd in range(C_HEADS):
        cols = slice(hd * C_DIM, (hd + 1) * C_DIM)
        ks_ref[hd] = kc[:, cols]
        vs_ref[hd] = vc[:, cols]
        for t0 in range(0, tm, tkc):
            kp = jnp.concatenate([ks_ref[hd, pl.ds(t0 + i, 8, stride=m), :] for i in range(m)], axis=0)
            kcp_ref[t0:t0 + tkc, cols] = kp.astype(BF16)
            vp = jnp.concatenate([vs_ref[hd, pl.ds(t0 + i, 8, stride=m), :] for i in range(m)], axis=0)
            vctp_ref[cols, t0:t0 + tkc] = vp.T.astype(BF16)


def _in_proj(x, cos_t, sin_t, g_attn, g_q, g_kv, w_ext, *, tm, tkc):
    nb, s, d = x.shape
    grid = (nb, s // tm)
    tok = lambda c, dt: jax.ShapeDtypeStruct((nb, s, c), dt)
    tr = lambda c: jax.ShapeDtypeStruct((nb, c, s), BF16)
    out_shape = (tok(512, BF16), tok(512, F32), tok(512, BF16), tok(A_ROPE, F32), tok(128, BF16),
                 tok(512, BF16), tok(512, F32), tok(512, BF16), tok(512, F32), tr(512),
                 tok(512, BF16), tok(512, F32), tok(512, BF16), tok(512, F32), tr(512))
    tspec = lambda c: pl.BlockSpec((None, tm, c), lambda b, i: (b, i, 0))
    trspec = pl.BlockSpec((None, 512, tm), lambda b, i: (b, 0, i))
    out_specs = (tspec(512), tspec(512), tspec(512), tspec(A_ROPE), tspec(128),
                 tspec(512), tspec(512), tspec(512), tspec(512), trspec,
                 tspec(512), tspec(512), tspec(512), tspec(512), trspec)
    vec = lambda n: _resident((1, n), lambda b, i: (0, 0))
    in_specs = [tspec(d),
                pl.BlockSpec((tm, 128), lambda b, i: (i, 0)),
                pl.BlockSpec((tm, 128), lambda b, i: (i, 0)),
                vec(d), vec(512), vec(512),
                _resident((d, IN_COLS), lambda b, i: (0, 0))]
    return pl.pallas_call(
        functools.partial(_in_proj_kernel, tkc=tkc),
        out_shape=out_shape, grid=grid, in_specs=in_specs, out_specs=out_specs,
        scratch_shapes=[pltpu.VMEM((C_HEADS, tm, C_DIM), F32), pltpu.VMEM((C_HEADS, tm, C_DIM), F32)],
        compiler_params=_cparams(("parallel", "parallel")), name="in_proj",
    )(x, cos_t, sin_t, g_attn, g_q, g_kv, w_ext)


def _q_proj_a_kernel(hq_ref, cos_ref, sin_ref, w_ref, q_ref):
    ql = jnp.dot(hq_ref[...], w_ref[...], preferred_element_type=F32)
    cos, sin = cos_ref[...], sin_ref[...]
    sc = A_SCALE * LOG2E
    for hd in range(A_HEADS):
        lo = hd * A_QK_PAD
        rot = ql[:, lo + 128:lo + 256] * cos + ql[:, 1024 + hd * 128:1024 + (hd + 1) * 128] * sin
        q_ref[:, lo:lo + 128] = (ql[:, lo:lo + 128] * sc).astype(BF16)
        q_ref[:, lo + 128:lo + 256] = (rot * sc).astype(BF16)


def _q_proj_a(hq, cos_t, sin_t, w_uq_ext, *, tm):
    nb, s, _ = hq.shape
    return pl.pallas_call(
        _q_proj_a_kernel,
        out_shape=jax.ShapeDtypeStruct((nb, s, A_HEADS * A_QK_PAD), BF16),
        grid=(nb, s // tm),
        in_specs=[pl.BlockSpec((None, tm, 512), lambda b, i: (b, i, 0)),
                  pl.BlockSpec((tm, 128), lambda b, i: (i, 0)),
                  pl.BlockSpec((tm, 128), lambda b, i: (i, 0)),
                  _resident((512, 1536), lambda b, i: (0, 0))],
        out_specs=pl.BlockSpec((None, tm, A_HEADS * A_QK_PAD), lambda b, i: (b, i, 0)),
        compiler_params=_cparams(("parallel", "parallel")), name="q_proj_a",
    )(hq, cos_t, sin_t, w_uq_ext)


def _kv_proj_a_kernel(ckv_ref, krp_ref, wk_ref, wv_ref, k_ref, vt_ref):
    c = ckv_ref[...]
    kn = jnp.dot(c, wk_ref[...], preferred_element_type=F32)
    v = jnp.dot(c, wv_ref[...], preferred_element_type=F32)
    krp = krp_ref[...]
    for hd in range(A_HEADS):
        lo = hd * A_QK_PAD
        k_ref[:, lo:lo + 128] = kn[:, hd * 128:(hd + 1) * 128].astype(BF16)
        k_ref[:, lo + 128:lo + 256] = krp
    vt_ref[...] = v.T.astype(BF16)


def _kv_proj_a(ckvb, krp, w_uk, w_uv, *, tm):
    nb, s, _ = ckvb.shape
    return pl.pallas_call(
        _kv_proj_a_kernel,
        out_shape=(jax.ShapeDtypeStruct((nb, s, A_HEADS * A_QK_PAD), BF16),
                   jax.ShapeDtypeStruct((nb, A_HEADS * A_VDIM, s), BF16)),
        grid=(nb, s // tm),
        in_specs=[pl.BlockSpec((None, tm, 512), lambda b, i: (b, i, 0)),
                  pl.BlockSpec((None, tm, 128), lambda b, i: (b, i, 0)),
                  _resident((512, 512), lambda b, i: (0, 0)),
                  _resident((512, 512), lambda b, i: (0, 0))],
        out_specs=(pl.BlockSpec((None, tm, A_HEADS * A_QK_PAD), lambda b, i: (b, i, 0)),
                   pl.BlockSpec((None, A_HEADS * A_VDIM, tm), lambda b, i: (b, 0, i))),
        compiler_params=_cparams(("parallel", "parallel")), name="kv_proj_a",
    )(ckvb, krp, w_uk, w_uv)


def _attn_a_kernel(q_ref, k_ref, vt_ref, g_ref, o_ref, s0_ref, s1_ref, m_ref, l_ref, acc_ref, *, tq, tk, n_past, nh):
    qi = pl.program_id(2)
    q0 = n_past + qi * tq
    n_full = q0 // tk
    m_ref[...] = jnp.full(m_ref.shape, -jnp.inf, F32)
    l_ref[...] = jnp.zeros(l_ref.shape, F32)
    acc_ref[...] = jnp.zeros(acc_ref.shape, F32)

    def scores(j, s_ref):
        start = pl.multiple_of(j * tk, tk)
        for hd in range(nh):
            qk = slice(hd * A_QK_PAD, (hd + 1) * A_QK_PAD)
            s_ref[hd] = lax.dot_general(k_ref[pl.ds(start, tk), qk], q_ref[:, qk], _NT, preferred_element_type=F32)

    def softmax_pv(j, s_ref, masked):
        start = pl.multiple_of(j * tk, tk)
        if masked:
            kpos = start + lax.broadcasted_iota(jnp.int32, (tk, tq), 0)
            qpos = q0 + lax.broadcasted_iota(jnp.int32, (tk, tq), 1)
            visible = (kpos >> 6) <= (qpos >> 6)
        for hd in range(nh):
            vv = slice(hd * A_VDIM, (hd + 1) * A_VDIM)
            s = s_ref[hd]
            if masked:
                s = jnp.where(visible, s, -jnp.inf)
            m_old = m_ref[hd]
            m_new = jnp.maximum(m_old, jnp.max(s, axis=0, keepdims=True))
            p = jnp.exp2(s - m_new)
            alpha = jnp.exp2(m_old - m_new)
            l_ref[hd] = alpha * l_ref[hd] + jnp.sum(p, axis=0, keepdims=True)
            pv = jnp.dot(vt_ref[vv, pl.ds(start, tk)], p.astype(BF16), preferred_element_type=F32)
            acc_ref[vv, :] = alpha * acc_ref[vv, :] + pv
            m_ref[hd] = m_new

    def pair_body(t, c):
        j = 2 * t
        scores(j + 1, s1_ref)
        softmax_pv(j, s0_ref, False)
        scores(j + 2, s0_ref)
        softmax_pv(j + 1, s1_ref, False)
        return c

    scores(0, s0_ref)
    lax.fori_loop(0, n_full // 2, pair_body, 0)
    scores(n_full + 1, s1_ref)
    softmax_pv(n_full, s0_ref, True)
    softmax_pv(n_full + 1, s1_ref, True)
    for hd in range(nh):
        vv = slice(hd * A_VDIM, (hd + 1) * A_VDIM)
        o = acc_ref[vv, :] / l_ref[hd]
        o_ref[vv, :] = _rms_cols(o, g_ref[vv, :]).astype(BF16)


def _attn_a(q, k, vt, g_col, *, tq, tk, n_past, nh):
    nb, sq, _ = q.shape
    sk = k.shape[1]
    assert tq == 2 * tk and n_past % tq == 0 and sq % tq == 0 and sk >= n_past + sq
    return pl.pallas_call(
        functools.partial(_attn_a_kernel, tq=tq, tk=tk, n_past=n_past, nh=nh),
        out_shape=jax.ShapeDtypeStruct((nb, A_HEADS * A_VDIM, sq), BF16),
        grid=(nb, A_HEADS // nh, sq // tq),
        in_specs=[pl.BlockSpec((None, tq, nh * A_QK_PAD), lambda b, h, i: (b, i, h)),
                  _resident((None, sk, nh * A_QK_PAD), lambda b, h, i: (b, 0, h)),
                  _resident((None, nh * A_VDIM, sk), lambda b, h, i: (b, h, 0)),
                  pl.BlockSpec((nh * A_VDIM, 1), lambda b, h, i: (h, 0))],
        out_specs=pl.BlockSpec((None, nh * A_VDIM, tq), lambda b, h, i: (b, h, i)),
        scratch_shapes=[pltpu.VMEM((nh, tk, tq), F32), pltpu.VMEM((nh, tk, tq), F32),
                        pltpu.VMEM((nh, 1, tq), F32), pltpu.VMEM((nh, 1, tq), F32),
                        pltpu.VMEM((nh * A_VDIM, tq), F32)],
        compiler_params=_cparams(("parallel", "parallel", "arbitrary")), name="attn_a",
    )(q, k, vt, g_col)


def _attn_b_kernel(q_ref, k0_ref, k1_ref, k2_ref, v0_ref, v1_ref, v2_ref, bias_ref, g_ref, o_ref, *, hist0):
    st = pl.program_id(1)
    thr = B_PAST - hist0 - st * B_QTILE
    row_ok = lax.broadcasted_iota(jnp.int32, (B_WIN, B_QTILE), 0) >= thr
    for hd in range(B_HEADS):
        cols = slice(hd * B_DIM, (hd + 1) * B_DIM)
        k = jnp.concatenate([k0_ref[:, cols], k1_ref[:, cols], k2_ref[:, cols]], axis=0)
        s = lax.dot_general(k, q_ref[:, cols], _NT, preferred_element_type=F32) + bias_ref[hd]
        s = jnp.where(row_ok, s, -jnp.inf)
        m = jnp.max(s, axis=0, keepdims=True)
        p = jnp.exp2(s - m)
        l = jnp.sum(p, axis=0, keepdims=True)
        v = jnp.concatenate([v0_ref[cols, :], v1_ref[cols, :], v2_ref[cols, :]], axis=1)
        o = jnp.dot(v, p.astype(BF16), preferred_element_type=F32) / l
        o_ref[cols, :] = _rms_cols(o, g_ref[cols, :]).astype(BF16)


def _attn_b(q, k_pad, vt_pad, bias, g_col, *, hist0):
    nb, sq, w = q.shape
    kspec = lambda o: pl.BlockSpec((None, B_QTILE, w), lambda b, i: (b, i + o, 0))
    vspec = lambda o: pl.BlockSpec((None, w, B_QTILE), lambda b, i: (b, 0, i + o))
    return pl.pallas_call(
        functools.partial(_attn_b_kernel, hist0=hist0),
        out_shape=jax.ShapeDtypeStruct((nb, w, sq), BF16),
        grid=(nb, sq // B_QTILE),
        in_specs=[pl.BlockSpec((None, B_QTILE, w), lambda b, i: (b, i, 0)),
                  kspec(0), kspec(1), kspec(2), vspec(0), vspec(1), vspec(2),
                  pl.BlockSpec((B_HEADS, B_WIN, B_QTILE), lambda b, i: (0, 0, 0)),
                  pl.BlockSpec((w, 1), lambda b, i: (0, 0))],
        out_specs=pl.BlockSpec((None, w, B_QTILE), lambda b, i: (b, 0, i)),
        compiler_params=_cparams(("parallel", "arbitrary")), name="attn_b",
    )(q, k_pad, k_pad, k_pad, vt_pad, vt_pad, vt_pad, bias, g_col)


def _band_bias_table(rel_bias):
    nh = rel_bias.shape[0]
    ring = 1024
    n_lo = B_WIN - 1 - (B_PAST + B_REL_CLIP)
    n_hi = B_PAST - B_REL_CLIP - 1
    lo = jnp.broadcast_to(rel_bias[:, :1], (nh, n_lo))
    hi = lambda n: jnp.broadcast_to(rel_bias[:, -1:], (nh, n))
    gap = ring - (B_QTILE + n_lo + rel_bias.shape[1] + n_hi)
    v = jnp.concatenate([hi(B_QTILE), jnp.zeros((nh, gap), rel_bias.dtype), lo, rel_bias, hi(n_hi)], axis=1)
    tbl = jnp.tile(v, (1, B_WIN))[:, :B_WIN * (ring - 1)].reshape(nh, B_WIN, ring - 1)[:, :, :B_QTILE] * LOG2E
    kc = np.arange(B_WIN)[:, None] // CHUNK
    qc = np.arange(B_QTILE)[None, :] // CHUNK
    valid = (kc >= qc) & (kc <= qc + B_PREV_CHUNKS)
    return jnp.where(valid[None], tbl, -jnp.inf)


def _attn_c_kernel(q_ref, kp_ref, vtp_ref, g_ref, o_ref, c_ref, acc_ref, *, tq, tk, n_past, nh):
    qi = pl.program_id(1)
    q0 = n_past + qi * tq
    n_full = q0 // tk
    n_mask = tq // tk
    m = tk // 8
    c_ref[...] = jnp.zeros(c_ref.shape, F32)
    acc_ref[...] = jnp.zeros(acc_ref.shape, F32)
    sub = lax.broadcasted_iota(jnp.int32, (8, tq), 0)

    def tile(j, masked):
        start = pl.multiple_of(j * tk, tk)
        zs = []
        for hd in range(nh):
            cols = slice(hd * C_DIM, (hd + 1) * C_DIM)
            zs.append(lax.dot_general(kp_ref[pl.ds(start, tk), cols], q_ref[:, cols], _NT,
                                      preferred_element_type=F32))
        if masked:
            r = lax.broadcasted_iota(jnp.int32, (tk, tq), 0)
            kpos = start + (r & 7) * m + (r >> 3)
            qpos = q0 + lax.broadcasted_iota(jnp.int32, (tk, tq), 1)
            causal = kpos < qpos
        cmin = None
        for hd in range(nh):
            cols = slice(hd * C_DIM, (hd + 1) * C_DIM)
            z = zs[hd]
            sp = jnp.maximum(z, 0.0) + jnp.log2(1.0 + jnp.exp2(-jnp.abs(z)))
            if masked:
                sp = jnp.where(causal, sp, 0.0)
            run = [None] * m
            run[m - 1] = sp[(m - 1) * 8:m * 8]
            for i in range(m - 2, -1, -1):
                run[i] = sp[i * 8:(i + 1) * 8] + run[i + 1]
            tot = run[0]
            incl = tot
            for sh in (1, 2, 4):
                incl = incl + jnp.where(sub + sh < 8, pltpu.roll(incl, 8 - sh, 0), 0.0)
            c_in = c_ref[hd]
            off = (incl - tot) + c_in
            rows = []
            for i in range(m):
                w = jnp.exp2(z[i * 8:(i + 1) * 8] - (run[i] + off))
                if masked:
                    w = jnp.where(causal[i * 8:(i + 1) * 8], w, 0.0)
                rows.append(w)
            w = jnp.concatenate(rows, axis=0).astype(BF16)
            acc_ref[cols, :] += jnp.dot(vtp_ref[cols, pl.ds(start, tk)], w, preferred_element_type=F32)
            c_new = c_in + incl[0:1, :]
            c_ref[hd] = c_new
            hmin = jnp.min(c_new)
            cmin = hmin if cmin is None else jnp.minimum(cmin, hmin)
        return cmin

    cmin = jnp.float32(0.0)
    for jj in range(n_mask - 1, -1, -1):
        cmin = tile(n_full + jj, True)

    def cond(carry):
        j, cm = carry
        return jnp.logical_and(j >= 0, cm < SB_EXIT_LOG2)

    def body(carry):
        j, _ = carry
        return j - 1, tile(j, False)

    lax.while_loop(cond, body, (n_full - 1, cmin))
    for hd in range(nh):
        cols = slice(hd * C_DIM, (hd + 1) * C_DIM)
        o_ref[cols, :] = _rms_cols(acc_ref[cols, :], g_ref[cols, :]).astype(BF16)


def _attn_c(q, kp, vtp, g_col, *, tq, tk, n_past):
    nb, sq, w = q.shape
    sk = kp.shape[1]
    nh = w // C_DIM
    return pl.pallas_call(
        functools.partial(_attn_c_kernel, tq=tq, tk=tk, n_past=n_past, nh=nh),
        out_shape=jax.ShapeDtypeStruct((nb, w, sq), BF16),
        grid=(nb, sq // tq),
        in_specs=[pl.BlockSpec((None, tq, w), lambda b, i: (b, i, 0)),
                  _resident((None, sk, w), lambda b, i: (b, 0, 0)),
                  _resident((None, w, sk), lambda b, i: (b, 0, 0)),
                  pl.BlockSpec((w, 1), lambda b, i: (0, 0))],
        out_specs=pl.BlockSpec((None, w, tq), lambda b, i: (b, 0, i)),
        scratch_shapes=[pltpu.VMEM((nh, 1, tq), F32), pltpu.VMEM((w, tq), F32)],
        compiler_params=_cparams(("parallel", "arbitrary")), name="attn_c",
    )(q, kp, vtp, g_col)


def _permute_keys(t, tk):
    nb, s, c = t.shape
    m = tk // 8
    return t.reshape(nb, s // tk, 8, m, c).transpose(0, 1, 3, 2, 4).reshape(nb, s, c)


def _out_proj_kernel(ma_ref, mb_ref, mc_ref, x_ref, w_ref, g_ref, x2_ref, h2_ref):
    y = lax.dot_general(ma_ref[...], w_ref[0:512, :], _TN, preferred_element_type=F32)
    y = y + lax.dot_general(mb_ref[...], w_ref[512:1024, :], _TN, preferred_element_type=F32)
    y = y + lax.dot_general(mc_ref[...], w_ref[1024:1536, :], _TN, preferred_element_type=F32)
    x2 = x_ref[...] + y
    x2_ref[...] = x2
    h2_ref[...] = _rms_rows(x2, g_ref[...]).astype(BF16)


def _out_proj(ma, mb, mc, x, w_out, g_ffn, *, tm):
    nb, s, d = x.shape
    mspec = pl.BlockSpec((None, 512, tm), lambda b, i: (b, 0, i))
    xspec = pl.BlockSpec((None, tm, d), lambda b, i: (b, i, 0))
    return pl.pallas_call(
        _out_proj_kernel,
        out_shape=(jax.ShapeDtypeStruct((nb, s, d), F32), jax.ShapeDtypeStruct((nb, s, d), BF16)),
        grid=(nb, s // tm),
        in_specs=[mspec, mspec, mspec, xspec,
                  _resident((1536, d), lambda b, i: (0, 0)),
                  _resident((1, d), lambda b, i: (0, 0))],
        out_specs=(xspec, xspec),
        compiler_params=_cparams(("parallel", "parallel")), name="out_proj",
    )(ma, mb, mc, x, w_out, g_ffn)


def _ffn_kernel(h_ref, x_ref, wa_ref, wg_ref, wca_ref, wcg_ref, ba_ref, bg_ref, wd_ref, pa_ref, pg_ref,
                o_ref, ca_ref, cg_ref, carry_ref, *, seq, n_seq, fc):
    i = pl.program_id(1)
    f = pl.program_id(2)

    @pl.when(f == 0)
    def _():
        o_ref[...] = x_ref[...]

    if n_seq == 1:
        @pl.when(i == 0)
        def _():
            carry_ref[f, 0:2, :] = pa_ref[...]
            carry_ref[f, 2:4, :] = pg_ref[...]

    h = h_ref[...]
    hw = fc // 2
    r8 = lax.broadcasted_iota(jnp.int32, (8, hw), 0)

    def conv(u, prev, wc, b):
        r1 = pltpu.roll(u, 1, 0)
        r2 = pltpu.roll(u, 2, 0)
        p1, p2 = [], []
        for sq in range(n_seq):
            lo = sq * seq
            pr = prev[sq]
            p1 += [jnp.where(r8 == 0, pr[1:2, :], r1[lo:lo + 8]), r1[lo + 8:lo + seq]]
            p2 += [jnp.where(r8 == 0, pr[0:1, :], jnp.where(r8 == 1, pr[1:2, :], r2[lo:lo + 8])), r2[lo + 8:lo + seq]]
        u1 = jnp.concatenate(p1, axis=0)
        u2 = jnp.concatenate(p2, axis=0)
        return b + wc[0:1, :] * u2 + wc[1:2, :] * u1 + wc[2:3, :] * u

    acts = []
    for half in range(2):
        cols = slice(half * hw, (half + 1) * hw)
        ua = jnp.dot(h, wa_ref[:, cols], preferred_element_type=F32)
        ug = jnp.dot(h, wg_ref[:, cols], preferred_element_type=F32)
        if n_seq == 1:
            prev_a = [carry_ref[f, 0:2, cols]]
            prev_g = [carry_ref[f, 2:4, cols]]
        else:
            prev_a = [pa_ref[sq, :, cols] for sq in range(n_seq)]
            prev_g = [pg_ref[sq, :, cols] for sq in range(n_seq)]
        a = conv(ua, prev_a, wca_ref[:, cols], ba_ref[:, cols])
        g = conv(ug, prev_g, wcg_ref[:, cols], bg_ref[:, cols])
        acts.append((a * (1.0 / (1.0 + jnp.exp(-a))) * g).astype(BF16))
        for sq in range(n_seq):
            hi = (sq + 1) * seq
            if n_seq == 1:
                carry_ref[f, 0:2, cols] = ua[hi - 2:hi, :]
                carry_ref[f, 2:4, cols] = ug[hi - 2:hi, :]
                ca_ref[:, cols] = ua[hi - 2:hi, :]
                cg_ref[:, cols] = ug[hi - 2:hi, :]
            else:
                ca_ref[sq, :, cols] = ua[hi - 2:hi, :]
                cg_ref[sq, :, cols] = ug[hi - 2:hi, :]
    o_ref[...] += jnp.dot(jnp.concatenate(acts, axis=1), wd_ref[...], preferred_element_type=F32)


def _ffn(h2, x2, w_up, w_conv, b_conv, w_down, conv_past, *, tm, fc, seq):
    nb, s, d = x2.shape
    dff = w_down.shape[0]
    nf = dff // fc
    n_seq = tm // seq
    assert (n_seq == 1 or (nb == 1 and tm == s)) and fc % 256 == 0 and seq % 8 == 0 and seq > 8
    xspec = pl.BlockSpec((None, tm, d), lambda b, i, f: (b, i, 0))
    lo = lambda r: pl.BlockSpec((r, fc), lambda b, i, f: (0, f))
    hi = lambda r: pl.BlockSpec((r, fc), lambda b, i, f: (0, nf + f))
    if n_seq == 1:
        st_lo = pl.BlockSpec((None, CONV_W - 1, fc), lambda b, i, f: (b, 0, f))
        st_hi = pl.BlockSpec((None, CONV_W - 1, fc), lambda b, i, f: (b, 0, nf + f))
        st_out = pl.BlockSpec((None, None, CONV_W - 1, fc), lambda b, i, f: (b, i, 0, f))
        st_shape = jax.ShapeDtypeStruct((nb, s // tm, CONV_W - 1, dff), F32)
    else:
        st_lo = pl.BlockSpec((n_seq, CONV_W - 1, fc), lambda b, i, f: (0, 0, f))
        st_hi = pl.BlockSpec((n_seq, CONV_W - 1, fc), lambda b, i, f: (0, 0, nf + f))
        st_out = st_lo
        st_shape = jax.ShapeDtypeStruct((n_seq, CONV_W - 1, dff), F32)
    x3, ca, cg = pl.pallas_call(
        functools.partial(_ffn_kernel, seq=seq, n_seq=n_seq, fc=fc),
        out_shape=(jax.ShapeDtypeStruct((nb, s, d), F32), st_shape, st_shape),
        grid=(nb, s // tm, nf),
        in_specs=[xspec, xspec, lo(d), hi(d), lo(CONV_W), hi(CONV_W), lo(1), hi(1),
                  pl.BlockSpec((fc, d), lambda b, i, f: (f, 0)), st_lo, st_hi],
        out_specs=(xspec, st_out, st_out),
        scratch_shapes=[pltpu.VMEM((nf, 4, fc), F32)],
        compiler_params=_cparams(("parallel", "arbitrary", "arbitrary")), name="conv_ffn",
    )(h2, x2, w_up, w_up, w_conv, w_conv, b_conv, b_conv, w_down, conv_past, conv_past)
    if n_seq == 1:
        ca, cg = ca[:, -1], cg[:, -1]
    return x3, jnp.concatenate([ca, cg], axis=-1)


def _final_norm_kernel(x_ref, g_ref, o_ref):
    o_ref[...] = _rms_rows(x_ref[...], g_ref[...])


def _final_norm(x, g, *, tm):
    nb, s, d = x.shape
    spec = pl.BlockSpec((None, tm, d), lambda b, i: (b, i, 0))
    return pl.pallas_call(
        _final_norm_kernel, out_shape=jax.ShapeDtypeStruct(x.shape, F32), grid=(nb, s // tm),
        in_specs=[spec, pl.BlockSpec((1, d), lambda b, i: (0, 0))], out_specs=spec,
        compiler_params=_cparams(("parallel", "parallel")), name="final_norm",
    )(x, g)


def _rope_tables(pos):
    half = A_ROPE // 2
    inv = ROPE_THETA ** (-jnp.arange(half, dtype=F32) / half)
    ang = pos.astype(F32)[:, None] * inv[None, :]
    cos, sin = jnp.cos(ang), jnp.sin(ang)
    z = jnp.zeros((pos.shape[0], 128 - A_ROPE), F32)
    return jnp.concatenate([cos, cos, z], axis=1), jnp.concatenate([-sin, sin, z], axis=1)


def _swap_halves_cols(w):
    half = w.shape[1] // 2
    return jnp.concatenate([w[:, half:], w[:, :half]], axis=1)


def _prep_layer(g_attn, w_in, g_q, w_uq, g_kv, w_uk, w_uv, rel_bias, g_heads, w_out, g_ffn, w_up, w_conv, b_conv,
                w_down):
    d = w_in.shape[0]
    kr = w_in[:, 1024:1024 + A_ROPE]
    z64 = jnp.zeros((d, 64), F32)
    w_ext = jnp.concatenate([w_in[:, :1024], kr, z64, _swap_halves_cols(kr), z64, w_in[:, 1024 + A_ROPE:]], axis=1)
    zq = jnp.zeros((w_uq.shape[0], 64), F32)
    main, swapped = [], []
    for hd in range(A_HEADS):
        lo = hd * (A_NOPE + A_ROPE)
        rope_cols = w_uq[:, lo + A_NOPE:lo + A_NOPE + A_ROPE]
        main += [w_uq[:, lo:lo + A_NOPE], rope_cols, zq]
        swapped += [_swap_halves_cols(rope_cols), zq]
    w_uq_ext = jnp.concatenate(main + swapped, axis=1)
    g_col = g_heads.reshape(-1, 1)
    return dict(
        g_attn=g_attn.reshape(1, -1), g_q=g_q.reshape(1, -1), g_kv=g_kv.reshape(1, -1),
        w_ext=w_ext.astype(BF16), w_uq_ext=w_uq_ext.astype(BF16), w_uk=w_uk.astype(BF16), w_uv=w_uv.astype(BF16),
        bias=_band_bias_table(rel_bias), g_a=g_col[0:512], g_b=g_col[512:1024], g_c=g_col[1024:1536],
        w_out=w_out.astype(BF16), g_ffn=g_ffn.reshape(1, -1), w_up=w_up.astype(BF16), w_conv=w_conv,
        b_conv=b_conv.reshape(1, -1), w_down=w_down.astype(BF16))


def _tile(s, pref):
    t = min(s, pref)
    assert s % t == 0, (s, t)
    return t


def _layer_prompt(x, p, cos_t, sin_t):
    nb, s, d = x.shape
    tm = _tile(s, 256)
    tkc = tm
    (hq, ckv, ckvb, kr, krp, qb, kb, kbb, vb, vbt, qc, kc, kcp, vc, vctp) = _in_proj(
        x, cos_t, sin_t, p["g_attn"], p["g_q"], p["g_kv"], p["w_ext"], tm=tm, tkc=tkc)
    q_a = _q_proj_a(hq, cos_t, sin_t, p["w_uq_ext"], tm=_tile(s, 512))
    k_a, vt_a = _kv_proj_a(ckvb, krp, p["w_uk"], p["w_uv"], tm=_tile(s, 512))
    ta = _tile(s, 1024)
    mix_a = _attn_a(q_a, k_a, vt_a, p["g_a"], tq=ta, tk=ta // 2, n_past=0, nh=2)
    k_pad = jnp.pad(kbb, ((0, 0), (B_PAST, 0), (0, 0)))
    vt_pad = jnp.pad(vbt, ((0, 0), (0, 0), (B_PAST, 0)))
    mix_b = _attn_b(qb, k_pad, vt_pad, p["bias"], p["g_b"], hist0=0)
    mix_c = _attn_c(qc, kcp, vctp, p["g_c"], tq=_tile(s, 512), tk=tkc, n_past=0)
    x2, h2 = _out_proj(mix_a, mix_b, mix_c, x, p["w_out"], p["g_ffn"], tm=_tile(s, 512))
    conv0 = jnp.zeros((nb, CONV_W - 1, p["w_up"].shape[1]), F32)
    tf = _tile(s, 512)
    x3, conv_new = _ffn(h2, x2, p["w_up"], p["w_conv"], p["b_conv"], p["w_down"], conv0, tm=tf, fc=512, seq=tf)
    rows = min(B_PAST, s)
    state = (ckv, kr, kb[:, s - rows:].reshape(nb, rows, B_HEADS, B_DIM), vb[:, s - rows:].reshape(nb, rows, B_HEADS, B_DIM),
             kc.reshape(nb, s, C_HEADS, C_DIM), vc.reshape(nb, s, C_HEADS, C_DIM), conv_new)
    return x3, state


def _pad_rows(t, n):
    return jnp.pad(t, ((0, 0), (0, n - t.shape[1]), (0, 0)))


def _layer_sample(x, p, cos_t, sin_t, ckv_past, kr_past, bk_past, bv_past, ck_past, cv_past, conv_past):
    nb, s, d = x.shape
    t = nb * s
    n_past = ckv_past.shape[1]
    xf = x.reshape(1, t, d)
    tm = _tile(t, 256)
    (hq, ckv, _, kr, krp, qb, kb, _, vb, _, qc, kc, _, vc, _) = _in_proj(
        xf, cos_t, sin_t, p["g_attn"], p["g_q"], p["g_kv"], p["w_ext"], tm=tm, tkc=tm)
    q_a = _q_proj_a(hq, cos_t, sin_t, p["w_uq_ext"], tm=tm)
    per = lambda a: a.reshape(nb, s, a.shape[-1])
    ckv, kr, kb, vb, kc, vc = per(ckv), per(kr), per(kb), per(vb), per(kc), per(vc)
    tqa = 256
    ska = n_past + tqa
    ckv_all = _pad_rows(jnp.concatenate([ckv_past, ckv], axis=1), ska).astype(BF16)
    kr_all = jnp.concatenate([kr_past, kr], axis=1)
    krp_all = _pad_rows(jnp.pad(kr_all, ((0, 0), (0, 0), (0, 128 - A_ROPE))), ska).astype(BF16)
    k_a, vt_a = _kv_proj_a(ckv_all, krp_all, p["w_uk"], p["w_uv"], tm=128)
    mix_a = _attn_a(_pad_rows(per(q_a), tqa), k_a, vt_a, p["g_a"], tq=tqa, tk=tqa // 2, n_past=n_past, nh=2)
    tq = 128
    sk = n_past + tq
    kb_all = _pad_rows(jnp.concatenate([bk_past.reshape(nb, -1, 512), kb], axis=1), B_WIN).astype(BF16)
    vb_all = _pad_rows(jnp.concatenate([bv_past.reshape(nb, -1, 512), vb], axis=1), B_WIN).astype(BF16)
    mix_b = _attn_b(_pad_rows(per(qb), B_QTILE), kb_all, vb_all.transpose(0, 2, 1), p["bias"], p["g_b"],
                    hist0=bk_past.shape[1])
    kc_all = _pad_rows(jnp.concatenate([ck_past.reshape(nb, -1, 512), kc], axis=1), sk)
    vc_all = _pad_rows(jnp.concatenate([cv_past.reshape(nb, -1, 512), vc], axis=1), sk)
    kcp = _permute_keys(kc_all, 128).astype(BF16)
    vctp = _permute_keys(vc_all, 128).astype(BF16).transpose(0, 2, 1)
    mix_c = _attn_c(_pad_rows(per(qc), tq), kcp, vctp, p["g_c"], tq=tq, tk=128, n_past=n_past)
    flat = lambda mx: mx[:, :, :s].transpose(1, 0, 2).reshape(1, 512, t)
    x2, h2 = _out_proj(flat(mix_a), flat(mix_b), flat(mix_c), xf, p["w_out"], p["g_ffn"], tm=tm)
    x3, conv_new = _ffn(h2, x2, p["w_up"], p["w_conv"], p["b_conv"], p["w_down"], conv_past, tm=t, fc=512, seq=s)
    x3 = x3.reshape(nb, s, d)
    state = (ckv, kr, kb.reshape(nb, s, B_HEADS, B_DIM), vb.reshape(nb, s, B_HEADS, B_DIM),
             kc.reshape(nb, s, C_HEADS, C_DIM), vc.reshape(nb, s, C_HEADS, C_DIM), conv_new)
    return x3, state


def kernel(x_prompt, x_sample, cache_a_ckv, cache_a_krope, cache_b_k, cache_b_v, cache_c_k, cache_c_v, state_conv,
           g_attn, w_in, g_q, w_uq, g_kv, w_uk, w_uv, rel_bias, g_heads, w_out, g_ffn, w_up, w_conv, b_conv, w_down,
           g_final):
    depth = w_in.shape[0]
    n_tp = x_prompt.shape[1]
    nbs, n_ts, _ = x_sample.shape
    past_len = cache_c_k.shape[2]
    cos_p, sin_p = _rope_tables(jnp.arange(n_tp, dtype=jnp.int32))
    pos_s = jnp.tile(past_len + jnp.arange(n_ts, dtype=jnp.int32), nbs)
    cos_s, sin_s = _rope_tables(pos_s)

    xp, xs = x_prompt, x_sample
    p_layers, s_layers = [], []
    for l in range(depth):
        p = _prep_layer(g_attn[l], w_in[l], g_q[l], w_uq[l], g_kv[l], w_uk[l], w_uv[l], rel_bias[l], g_heads[l],
                        w_out[l], g_ffn[l], w_up[l], w_conv[l], b_conv[l], w_down[l])
        xp, p_st = _layer_prompt(xp, p, cos_p, sin_p)
        xs, s_st = _layer_sample(xs, p, cos_s, sin_s, cache_a_ckv[l], cache_a_krope[l], cache_b_k[l], cache_b_v[l],
                                 cache_c_k[l], cache_c_v[l], state_conv[l])
        p_layers.append(p_st)
        s_layers.append(s_st)

    gf = g_final.reshape(1, -1)
    y_prompt = _final_norm(xp, gf, tm=_tile(n_tp, 512))
    y_sample = _final_norm(xs, gf, tm=n_ts)
    p_out = [jnp.stack(t) for t in zip(*p_layers)]
    s_out = [jnp.stack(t) for t in zip(*s_layers)]
    return (y_prompt, y_sample, *p_out, *s_out)
```

```python
import functools
import math

import numpy as np
import jax
import jax.numpy as jnp
from jax import lax
from jax.experimental import pallas as pl
from jax.experimental.pallas import tpu as pltpu

F32 = jnp.float32
BF16 = jnp.bfloat16

EPS = 1e-6
CHUNK = 64
ROPE_THETA = 10000.0
A_HEADS, A_NOPE, A_ROPE, A_VDIM = 4, 128, 64, 128
A_QK_PAD = 256
B_HEADS, B_DIM, B_PREV_CHUNKS, B_REL_CLIP = 4, 128, 8, 128
B_PAST = B_PREV_CHUNKS * CHUNK
C_HEADS, C_DIM = 4, 128
HEAD_DIM = 128
CONV_W = 3
LOG2E = 1.4426950408889634
A_SCALE = 1.0 / math.sqrt(A_NOPE + A_ROPE)
B_SCALE = 1.0 / math.sqrt(B_DIM)
C_SCALE = 1.0 / math.sqrt(C_DIM)

SB_EXIT_LOG2 = 160.0

B_QTILE = 4 * CHUNK
B_WIN = (B_PREV_CHUNKS + 4) * CHUNK

FFN_CHUNK = 512

VMEM_LIMIT = 56 * 1024 * 1024


def _cparams(sem):
    return pltpu.CompilerParams(dimension_semantics=sem, vmem_limit_bytes=VMEM_LIMIT)


def _rms_rows(x, g):
    ms = jnp.mean(x * x, axis=-1, keepdims=True)
    return x * lax.rsqrt(ms + EPS) * g


def _rms_cols(o, g):
    ms = jnp.mean(o * o, axis=0, keepdims=True)
    return o * lax.rsqrt(ms + EPS) * g


_NT = (((1,), (1,)), ((), ()))
_TN = (((0,), (0,)), ((), ()))


def _resident(shape, index_map):
    return pl.BlockSpec(shape, index_map, pipeline_mode=pl.Buffered(1))


IN_COLS = 512 + 512 + 256 + 1536 + 1536


def _in_proj_kernel(x_ref, cos_ref, sin_ref, gattn_ref, gq_ref, gkv_ref, w_ref,
                    hq_ref, ckv_ref, ckvb_ref, kr_ref, krp_ref,
                    qb_ref, kb_ref, kbb_ref, vb_ref, vbt_ref,
                    qc_ref, kc_ref, kcp_ref, vc_ref, vctp_ref, ks_ref, vs_ref, *, tkc):
    h = _rms_rows(x_ref[...], gattn_ref[...]).astype(BF16)

    def proj(lo, n):
        return jnp.dot(h, w_ref[:, lo:lo + n], preferred_element_type=F32)

    hq_ref[...] = _rms_rows(proj(0, 512), gq_ref[...]).astype(BF16)
    ckv = _rms_rows(proj(512, 512), gkv_ref[...])
    ckv_ref[...] = ckv
    ckvb_ref[...] = ckv.astype(BF16)
    kr2 = proj(1024, 256)
    krp = kr2[:, :128] * cos_ref[...] + kr2[:, 128:] * sin_ref[...]
    krp_ref[...] = krp.astype(BF16)
    kr_ref[...] = krp[:, :A_ROPE]
    o = 1280
    qb_ref[...] = (proj(o, 512) * (B_SCALE * LOG2E)).astype(BF16)
    kb = proj(o + 512, 512)
    kb_ref[...] = kb
    kbb_ref[...] = kb.astype(BF16)
    vb = proj(o + 1024, 512)
    vb_ref[...] = vb
    vbt_ref[...] = vb.T.astype(BF16)
    o = 2816
    qc_ref[...] = (proj(o, 512) * (C_SCALE * LOG2E)).astype(BF16)
    kc = proj(o + 512, 512)
    vc = proj(o + 1024, 512)
    kc_ref[...] = kc
    vc_ref[...] = vc
    tm = kc_ref.shape[0]
    m = tkc // 8
    for hd in range(C_HEADS):
        cols = slice(hd * C_DIM, (hd + 1) * C_DIM)
        ks_ref[hd] = kc[:, cols]
        vs_ref[hd] = vc[:, cols]
        for t0 in range(0, tm, tkc):
            kp = jnp.concatenate([ks_ref[hd, pl.ds(t0 + i, 8, stride=m), :] for i in range(m)], axis=0)
            kcp_ref[t0:t0 + tkc, cols] = kp.astype(BF16)
            vp = jnp.concatenate([vs_ref[hd, pl.ds(t0 + i, 8, stride=m), :] for i in range(m)], axis=0)
            vctp_ref[cols, t0:t0 + tkc] = vp.T.astype(BF16)


def _in_proj(x, cos_t, sin_t, g_attn, g_q, g_kv, w_ext, *, tm, tkc):
    nb, s, d = x.shape
    grid = (nb, s // tm)
    tok = lambda c, dt: jax.ShapeDtypeStruct((nb, s, c), dt)
    tr = lambda c: jax.ShapeDtypeStruct((nb, c, s), BF16)
    out_shape = (tok(512, BF16), tok(512, F32), tok(512, BF16), tok(A_ROPE, F32), tok(128, BF16),
                 tok(512, BF16), tok(512, F32), tok(512, BF16), tok(512, F32), tr(512),
                 tok(512, BF16), tok(512, F32), tok(512, BF16), tok(512, F32), tr(512))
    tspec = lambda c: pl.BlockSpec((None, tm, c), lambda b, i: (b, i, 0))
    trspec = pl.BlockSpec((None, 512, tm), lambda b, i: (b, 0, i))
    out_specs = (tspec(512), tspec(512), tspec(512), tspec(A_ROPE), tspec(128),
                 tspec(512), tspec(512), tspec(512), tspec(512), trspec,
                 tspec(512), tspec(512), tspec(512), tspec(512), trspec)
    vec = lambda n: _resident((1, n), lambda b, i: (0, 0))
    in_specs = [tspec(d),
                pl.BlockSpec((tm, 128), lambda b, i: (i, 0)),
                pl.BlockSpec((tm, 128), lambda b, i: (i, 0)),
                vec(d), vec(512), vec(512),
                _resident((d, IN_COLS), lambda b, i: (0, 0))]
    return pl.pallas_call(
        functools.partial(_in_proj_kernel, tkc=tkc),
        out_shape=out_shape, grid=grid, in_specs=in_specs, out_specs=out_specs,
        scratch_shapes=[pltpu.VMEM((C_HEADS, tm, C_DIM), F32), pltpu.VMEM((C_HEADS, tm, C_DIM), F32)],
        compiler_params=_cparams(("parallel", "parallel")), name="in_proj",
    )(x, cos_t, sin_t, g_attn, g_q, g_kv, w_ext)


def _q_proj_a_kernel(hq_ref, cos_ref, sin_ref, w_ref, q_ref):
    ql = jnp.dot(hq_ref[...], w_ref[...], preferred_element_type=F32)
    cos, sin = cos_ref[...], sin_ref[...]
    sc = A_SCALE * LOG2E
    for hd in range(A_HEADS):
        lo = hd * A_QK_PAD
        rot = ql[:, lo + 128:lo + 256] * cos + ql[:, 1024 + hd * 128:1024 + (hd + 1) * 128] * sin
        q_ref[:, lo:lo + 128] = (ql[:, lo:lo + 128] * sc).astype(BF16)
        q_ref[:, lo + 128:lo + 256] = (rot * sc).astype(BF16)


def _q_proj_a(hq, cos_t, sin_t, w_uq_ext, *, tm):
    nb, s, _ = hq.shape
    return pl.pallas_call(
        _q_proj_a_kernel,
        out_shape=jax.ShapeDtypeStruct((nb, s, A_HEADS * A_QK_PAD), BF16),
        grid=(nb, s // tm),
        in_specs=[pl.BlockSpec((None, tm, 512), lambda b, i: (b, i, 0)),
                  pl.BlockSpec((tm, 128), lambda b, i: (i, 0)),
                  pl.BlockSpec((tm, 128), lambda b, i: (i, 0)),
                  _resident((512, 1536), lambda b, i: (0, 0))],
        out_specs=pl.BlockSpec((None, tm, A_HEADS * A_QK_PAD), lambda b, i: (b, i, 0)),
        compiler_params=_cparams(("parallel", "parallel")), name="q_proj_a",
    )(hq, cos_t, sin_t, w_uq_ext)


def _kv_proj_a_kernel(ckv_ref, krp_ref, wk_ref, wv_ref, k_ref, vt_ref):
    c = ckv_ref[...]
    kn = jnp.dot(c, wk_ref[...], preferred_element_type=F32)
    v = jnp.dot(c, wv_ref[...], preferred_element_type=F32)
    krp = krp_ref[...]
    for hd in range(A_HEADS):
        lo = hd * A_QK_PAD
        k_ref[:, lo:lo + 128] = kn[:, hd * 128:(hd + 1) * 128].astype(BF16)
        k_ref[:, lo + 128:lo + 256] = krp
    vt_ref[...] = v.T.astype(BF16)


def _kv_proj_a(ckvb, krp, w_uk, w_uv, *, tm):
    nb, s, _ = ckvb.shape
    return pl.pallas_call(
        _kv_proj_a_kernel,
        out_shape=(jax.ShapeDtypeStruct((nb, s, A_HEADS * A_QK_PAD), BF16),
                   jax.ShapeDtypeStruct((nb, A_HEADS * A_VDIM, s), BF16)),
        grid=(nb, s // tm),
        in_specs=[pl.BlockSpec((None, tm, 512), lambda b, i: (b, i, 0)),
                  pl.BlockSpec((None, tm, 128), lambda b, i: (b, i, 0)),
                  _resident((512, 512), lambda b, i: (0, 0)),
                  _resident((512, 512), lambda b, i: (0, 0))],
        out_specs=(pl.BlockSpec((None, tm, A_HEADS * A_QK_PAD), lambda b, i: (b, i, 0)),
                   pl.BlockSpec((None, A_HEADS * A_VDIM, tm), lambda b, i: (b, 0, i))),
        compiler_params=_cparams(("parallel", "parallel")), name="kv_proj_a",
    )(ckvb, krp, w_uk, w_uv)


def _attn_a_kernel(q_ref, k_ref, vt_ref, g_ref, o_ref, s0_ref, s1_ref, m_ref, l_ref, acc_ref, *, tq, tk, n_past, nh):
    qi = pl.program_id(2)
    q0 = n_past + qi * tq
    n_full = q0 // tk
    m_ref[...] = jnp.full(m_ref.shape, -jnp.inf, F32)
    l_ref[...] = jnp.zeros(l_ref.shape, F32)
    acc_ref[...] = jnp.zeros(acc_ref.shape, F32)

    def scores(j, s_ref):
        start = pl.multiple_of(j * tk, tk)
        for hd in range(nh):
            qk = slice(hd * A_QK_PAD, (hd + 1) * A_QK_PAD)
            s_ref[hd] = lax.dot_general(k_ref[pl.ds(start, tk), qk], q_ref[:, qk], _NT, preferred_element_type=F32)

    def softmax_pv(j, s_ref, masked):
        start = pl.multiple_of(j * tk, tk)
        if masked:
            kpos = start + lax.broadcasted_iota(jnp.int32, (tk, tq), 0)
            qpos = q0 + lax.broadcasted_iota(jnp.int32, (tk, tq), 1)
            visible = (kpos >> 6) <= (qpos >> 6)
        for hd in range(nh):
            vv = slice(hd * A_VDIM, (hd + 1) * A_VDIM)
            s = s_ref[hd]
            if masked:
                s = jnp.where(visible, s, -jnp.inf)
            m_old = m_ref[hd]
            m_new = jnp.maximum(m_old, jnp.max(s, axis=0, keepdims=True))
            p = jnp.exp2(s - m_new)
            alpha = jnp.exp2(m_old - m_new)
            l_ref[hd] = alpha * l_ref[hd] + jnp.sum(p, axis=0, keepdims=True)
            pv = jnp.dot(vt_ref[vv, pl.ds(start, tk)], p.astype(BF16), preferred_element_type=F32)
            acc_ref[vv, :] = alpha * acc_ref[vv, :] + pv
            m_ref[hd] = m_new

    def pair_body(t, c):
        j = 2 * t
        scores(j + 1, s1_ref)
        softmax_pv(j, s0_ref, False)
        scores(j + 2, s0_ref)
        softmax_pv(j + 1, s1_ref, False)
        return c

    scores(0, s0_ref)
    lax.fori_loop(0, n_full // 2, pair_body, 0)
    scores(n_full + 1, s1_ref)
    softmax_pv(n_full, s0_ref, True)
    softmax_pv(n_full + 1, s1_ref, True)
    for hd in range(nh):
        vv = slice(hd * A_VDIM, (hd + 1) * A_VDIM)
        o = acc_ref[vv, :] / l_ref[hd]
        o_ref[vv, :] = _rms_cols(o, g_ref[vv, :]).astype(BF16)


def _attn_a(q, k, vt, g_col, *, tq, tk, n_past, nh):
    nb, sq, _ = q.shape
    sk = k.shape[1]
    assert tq == 2 * tk and n_past % tq == 0 and sq % tq == 0 and sk >= n_past + sq
    return pl.pallas_call(
        functools.partial(_attn_a_kernel, tq=tq, tk=tk, n_past=n_past, nh=nh),
        out_shape=jax.ShapeDtypeStruct((nb, A_HEADS * A_VDIM, sq), BF16),
        grid=(nb, A_HEADS // nh, sq // tq),
        in_specs=[pl.BlockSpec((None, tq, nh * A_QK_PAD), lambda b, h, i: (b, i, h)),
                  _resident((None, sk, nh * A_QK_PAD), lambda b, h, i: (b, 0, h)),
                  _resident((None, nh * A_VDIM, sk), lambda b, h, i: (b, h, 0)),
                  pl.BlockSpec((nh * A_VDIM, 1), lambda b, h, i: (h, 0))],
        out_specs=pl.BlockSpec((None, nh * A_VDIM, tq), lambda b, h, i: (b, h, i)),
        scratch_shapes=[pltpu.VMEM((nh, tk, tq), F32), pltpu.VMEM((nh, tk, tq), F32),
                        pltpu.VMEM((nh, 1, tq), F32), pltpu.VMEM((nh, 1, tq), F32),
                        pltpu.VMEM((nh * A_VDIM, tq), F32)],
        compiler_params=_cparams(("parallel", "parallel", "arbitrary")), name="attn_a",
    )(q, k, vt, g_col)


def _attn_b_kernel(q_ref, k0_ref, k1_ref, k2_ref, v0_ref, v1_ref, v2_ref, bias_ref, g_ref, o_ref, *, hist0):
    st = pl.program_id(1)
    thr = B_PAST - hist0 - st * B_QTILE
    row_ok = lax.broadcasted_iota(jnp.int32, (B_WIN, B_QTILE), 0) >= thr
    for hd in range(B_HEADS):
        cols = slice(hd * B_DIM, (hd + 1) * B_DIM)
        k = jnp.concatenate([k0_ref[:, cols], k1_ref[:, cols], k2_ref[:, cols]], axis=0)
        s = lax.dot_general(k, q_ref[:, cols], _NT, preferred_element_type=F32) + bias_ref[hd]
        s = jnp.where(row_ok, s, -jnp.inf)
        m = jnp.max(s, axis=0, keepdims=True)
        p = jnp.exp2(s - m)
        l = jnp.sum(p, axis=0, keepdims=True)
        v = jnp.concatenate([v0_ref[cols, :], v1_ref[cols, :], v2_ref[cols, :]], axis=1)
        o = jnp.dot(v, p.astype(BF16), preferred_element_type=F32) / l
        o_ref[cols, :] = _rms_cols(o, g_ref[cols, :]).astype(BF16)


def _attn_b(q, k_pad, vt_pad, bias, g_col, *, hist0):
    nb, sq, w = q.shape
    kspec = lambda o: pl.BlockSpec((None, B_QTILE, w), lambda b, i: (b, i + o, 0))
    vspec = lambda o: pl.BlockSpec((None, w, B_QTILE), lambda b, i: (b, 0, i + o))
    return pl.pallas_call(
        functools.partial(_attn_b_kernel, hist0=hist0),
        out_shape=jax.ShapeDtypeStruct((nb, w, sq), BF16),
        grid=(nb, sq // B_QTILE),
        in_specs=[pl.BlockSpec((None, B_QTILE, w), lambda b, i: (b, i, 0)),
                  kspec(0), kspec(1), kspec(2), vspec(0), vspec(1), vspec(2),
                  pl.BlockSpec((B_HEADS, B_WIN, B_QTILE), lambda b, i: (0, 0, 0)),
                  pl.BlockSpec((w, 1), lambda b, i: (0, 0))],
        out_specs=pl.BlockSpec((None, w, B_QTILE), lambda b, i: (b, 0, i)),
        compiler_params=_cparams(("parallel", "arbitrary")), name="attn_b",
    )(q, k_pad, k_pad, k_pad, vt_pad, vt_pad, vt_pad, bias, g_col)


def _band_bias_table(rel_bias):
    nh = rel_bias.shape[0]
    ring = 1024
    n_lo = B_WIN - 1 - (B_PAST + B_REL_CLIP)
    n_hi = B_PAST - B_REL_CLIP - 1
    lo = jnp.broadcast_to(rel_bias[:, :1], (nh, n_lo))
    hi = lambda n: jnp.broadcast_to(rel_bias[:, -1:], (nh, n))
    gap = ring - (B_QTILE + n_lo + rel_bias.shape[1] + n_hi)
    v = jnp.concatenate([hi(B_QTILE), jnp.zeros((nh, gap), rel_bias.dtype), lo, rel_bias, hi(n_hi)], axis=1)
    tbl = jnp.tile(v, (1, B_WIN))[:, :B_WIN * (ring - 1)].reshape(nh, B_WIN, ring - 1)[:, :, :B_QTILE] * LOG2E
    kc = np.arange(B_WIN)[:, None] // CHUNK
    qc = np.arange(B_QTILE)[None, :] // CHUNK
    valid = (kc >= qc) & (kc <= qc + B_PREV_CHUNKS)
    return jnp.where(valid[None], tbl, -jnp.inf)


def _attn_c_kernel(q_ref, kp_ref, vtp_ref, g_ref, o_ref, c_ref, acc_ref, *, tq, tk, n_past, nh):
    qi = pl.program_id(1)
    q0 = n_past + qi * tq
    n_full = q0 // tk
    n_mask = tq // tk
    m = tk // 8
    c_ref[...] = jnp.zeros(c_ref.shape, F32)
    acc_ref[...] = jnp.zeros(acc_ref.shape, F32)
    sub = lax.broadcasted_iota(jnp.int32, (8, tq), 0)

    def tile(j, masked):
        start = pl.multiple_of(j * tk, tk)
        zs = []
        for hd in range(nh):
            cols = slice(hd * C_DIM, (hd + 1) * C_DIM)
            zs.append(lax.dot_general(kp_ref[pl.ds(start, tk), cols], q_ref[:, cols], _NT,
                                      preferred_element_type=F32))
        if masked:
            r = lax.broadcasted_iota(jnp.int32, (tk, tq), 0)
            kpos = start + (r & 7) * m + (r >> 3)
            qpos = q0 + lax.broadcasted_iota(jnp.int32, (tk, tq), 1)
            causal = kpos < qpos
        cmin = None
        for hd in range(nh):
            cols = slice(hd * C_DIM, (hd + 1) * C_DIM)
            z = zs[hd]
            sp = jnp.maximum(z, 0.0) + jnp.log2(1.0 + jnp.exp2(-jnp.abs(z)))
            if masked:
                sp = jnp.where(causal, sp, 0.0)
            run = [None] * m
            run[m - 1] = sp[(m - 1) * 8:m * 8]
            for i in range(m - 2, -1, -1):
                run[i] = sp[i * 8:(i + 1) * 8] + run[i + 1]
            tot = run[0]
            incl = tot
            for sh in (1, 2, 4):
                incl = incl + jnp.where(sub + sh < 8, pltpu.roll(incl, 8 - sh, 0), 0.0)
            c_in = c_ref[hd]
            off = (incl - tot) + c_in
            rows = []
            for i in range(m):
                w = jnp.exp2(z[i * 8:(i + 1) * 8] - (run[i] + off))
                if masked:
                    w = jnp.where(causal[i * 8:(i + 1) * 8], w, 0.0)
                rows.append(w)
            w = jnp.concatenate(rows, axis=0).astype(BF16)
            acc_ref[cols, :] += jnp.dot(vtp_ref[cols, pl.ds(start, tk)], w, preferred_element_type=F32)
            c_new = c_in + incl[0:1, :]
            c_ref[hd] = c_new
            hmin = jnp.min(c_new)
            cmin = hmin if cmin is None else jnp.minimum(cmin, hmin)
        return cmin

    cmin = jnp.float32(0.0)
    for jj in range(n_mask - 1, -1, -1):
        cmin = tile(n_full + jj, True)

    def cond(carry):
        j, cm = carry
        return jnp.logical_and(j >= 0, cm < SB_EXIT_LOG2)

    def body(carry):
        j, _ = carry
        return j - 1, tile(j, False)

    lax.while_loop(cond, body, (n_full - 1, cmin))
    for hd in range(nh):
        cols = slice(hd * C_DIM, (hd + 1) * C_DIM)
        o_ref[cols, :] = _rms_cols(acc_ref[cols, :], g_ref[cols, :]).astype(BF16)


def _attn_c(q, kp, vtp, g_col, *, tq, tk, n_past):
    nb, sq, w = q.shape
    sk = kp.shape[1]
    nh = w // C_DIM
    return pl.pallas_call(
        functools.partial(_attn_c_kernel, tq=tq, tk=tk, n_past=n_past, nh=nh),
        out_shape=jax.ShapeDtypeStruct((nb, w, sq), BF16),
        grid=(nb, sq // tq),
        in_specs=[pl.BlockSpec((None, tq, w), lambda b, i: (b, i, 0)),
                  _resident((None, sk, w), lambda b, i: (b, 0, 0)),
                  _resident((None, w, sk), lambda b, i: (b, 0, 0)),
                  pl.BlockSpec((w, 1), lambda b, i: (0, 0))],
        out_specs=pl.BlockSpec((None, w, tq), lambda b, i: (b, 0, i)),
        scratch_shapes=[pltpu.VMEM((nh, 1, tq), F32), pltpu.VMEM((w, tq), F32)],
        compiler_params=_cparams(("parallel", "arbitrary")), name="attn_c",
    )(q, kp, vtp, g_col)


def _permute_keys(t, tk):
    nb, s, c = t.shape
    m = tk // 8
    return t.reshape(nb, s // tk, 8, m, c).transpose(0, 1, 3, 2, 4).reshape(nb, s, c)


def _out_proj_kernel(ma_ref, mb_ref, mc_ref, x_ref, w_ref, g_ref, x2_ref, h2_ref):
    y = lax.dot_general(ma_ref[...], w_ref[0:512, :], _TN, preferred_element_type=F32)
    y = y + lax.dot_general(mb_ref[...], w_ref[512:1024, :], _TN, preferred_element_type=F32)
    y = y + lax.dot_general(mc_ref[...], w_ref[1024:1536, :], _TN, preferred_element_type=F32)
    x2 = x_ref[...] + y
    x2_ref[...] = x2
    h2_ref[...] = _rms_rows(x2, g_ref[...]).astype(BF16)


def _out_proj(ma, mb, mc, x, w_out, g_ffn, *, tm):
    nb, s, d = x.shape
    mspec = pl.BlockSpec((None, 512, tm), lambda b, i: (b, 0, i))
    xspec = pl.BlockSpec((None, tm, d), lambda b, i: (b, i, 0))
    return pl.pallas_call(
        _out_proj_kernel,
        out_shape=(jax.ShapeDtypeStruct((nb, s, d), F32), jax.ShapeDtypeStruct((nb, s, d), BF16)),
        grid=(nb, s // tm),
        in_specs=[mspec, mspec, mspec, xspec,
                  _resident((1536, d), lambda b, i: (0, 0)),
                  _resident((1, d), lambda b, i: (0, 0))],
        out_specs=(xspec, xspec),
        compiler_params=_cparams(("parallel", "parallel")), name="out_proj",
    )(ma, mb, mc, x, w_out, g_ffn)


def _ffn_kernel(h_ref, x_ref, wa_ref, wg_ref, wca_ref, wcg_ref, ba_ref, bg_ref, wd_ref, pa_ref, pg_ref,
                o_ref, ca_ref, cg_ref, carry_ref, us_ref, *, seq, n_seq, fc):
    i = pl.program_id(1)
    f = pl.program_id(2)
    tm = n_seq * seq

    @pl.when(f == 0)
    def _():
        o_ref[...] = x_ref[...]

    if n_seq == 1:
        @pl.when(i == 0)
        def _():
            carry_ref[f, 0:2, :] = pa_ref[...]
            carry_ref[f, 2:4, :] = pg_ref[...]

    h = h_ref[...]
    hw = fc // 2
    r8 = lax.broadcasted_iota(jnp.int32, (8, hw), 0)

    def conv(slot, u, prev, wc, b):
        us_ref[slot, 6:8, :] = prev[0]
        us_ref[slot, 8:8 + tm, :] = u
        u1 = us_ref[slot, pl.ds(7, tm), :]
        u2 = us_ref[slot, pl.ds(6, tm), :]
        if n_seq > 1:
            p1, p2 = [], []
            for sq in range(n_seq):
                lo = sq * seq
                pr = prev[sq]
                p1 += [jnp.where(r8 == 0, pr[1:2, :], u1[lo:lo + 8]), u1[lo + 8:lo + seq]]
                p2 += [jnp.where(r8 == 0, pr[0:1, :], jnp.where(r8 == 1, pr[1:2, :], u2[lo:lo + 8])),
                       u2[lo + 8:lo + seq]]
            u1 = jnp.concatenate(p1, axis=0)
            u2 = jnp.concatenate(p2, axis=0)
        return b + wc[0:1, :] * u2 + wc[1:2, :] * u1 + wc[2:3, :] * u

    acts = []
    for half in range(2):
        cols = slice(half * hw, (half + 1) * hw)
        ua = jnp.dot(h, wa_ref[:, cols], preferred_element_type=F32)
        ug = jnp.dot(h, wg_ref[:, cols], preferred_element_type=F32)
        if n_seq == 1:
            prev_a = [carry_ref[f, 0:2, cols]]
            prev_g = [carry_ref[f, 2:4, cols]]
        else:
            prev_a = [pa_ref[sq, :, cols] for sq in range(n_seq)]
            prev_g = [pg_ref[sq, :, cols] for sq in range(n_seq)]
        a = conv(0, ua, prev_a, wca_ref[:, cols], ba_ref[:, cols])
        g = conv(1, ug, prev_g, wcg_ref[:, cols], bg_ref[:, cols])
        acts.append((a * (1.0 / (1.0 + jnp.exp(-a))) * g).astype(BF16))
        for sq in range(n_seq):
            hi = (sq + 1) * seq
            if n_seq == 1:
                carry_ref[f, 0:2, cols] = ua[hi - 2:hi, :]
                carry_ref[f, 2:4, cols] = ug[hi - 2:hi, :]
                ca_ref[:, cols] = ua[hi - 2:hi, :]
                cg_ref[:, cols] = ug[hi - 2:hi, :]
            else:
                ca_ref[sq, :, cols] = ua[hi - 2:hi, :]
                cg_ref[sq, :, cols] = ug[hi - 2:hi, :]
    o_ref[...] += jnp.dot(jnp.concatenate(acts, axis=1), wd_ref[...], preferred_element_type=F32)


def _ffn(h2, x2, w_up, w_conv, b_conv, w_down, conv_past, *, tm, seq):
    nb, s, d = x2.shape
    dff = w_down.shape[0]
    fc = w_up.shape[2]
    nf = dff // fc
    n_seq = tm // seq
    assert (n_seq == 1 or (nb == 1 and tm == s)) and fc % 256 == 0 and seq % 8 == 0 and seq > 8
    xspec = pl.BlockSpec((None, tm, d), lambda b, i, f: (b, i, 0))
    lo = lambda r: pl.BlockSpec((r, fc), lambda b, i, f: (0, f))
    hi = lambda r: pl.BlockSpec((r, fc), lambda b, i, f: (0, nf + f))
    up_lo = pl.BlockSpec((None, d, fc), lambda b, i, f: (f, 0, 0))
    up_hi = pl.BlockSpec((None, d, fc), lambda b, i, f: (nf + f, 0, 0))
    if n_seq == 1:
        st_lo = pl.BlockSpec((None, CONV_W - 1, fc), lambda b, i, f: (b, 0, f))
        st_hi = pl.BlockSpec((None, CONV_W - 1, fc), lambda b, i, f: (b, 0, nf + f))
        st_out = pl.BlockSpec((None, None, CONV_W - 1, fc), lambda b, i, f: (b, i, 0, f))
        st_shape = jax.ShapeDtypeStruct((nb, s // tm, CONV_W - 1, dff), F32)
    else:
        st_lo = pl.BlockSpec((n_seq, CONV_W - 1, fc), lambda b, i, f: (0, 0, f))
        st_hi = pl.BlockSpec((n_seq, CONV_W - 1, fc), lambda b, i, f: (0, 0, nf + f))
        st_out = st_lo
        st_shape = jax.ShapeDtypeStruct((n_seq, CONV_W - 1, dff), F32)
    x3, ca, cg = pl.pallas_call(
        functools.partial(_ffn_kernel, seq=seq, n_seq=n_seq, fc=fc),
        out_shape=(jax.ShapeDtypeStruct((nb, s, d), F32), st_shape, st_shape),
        grid=(nb, s // tm, nf),
        in_specs=[xspec, xspec, up_lo, up_hi, lo(CONV_W), hi(CONV_W), lo(1), hi(1),
                  pl.BlockSpec((fc, d), lambda b, i, f: (f, 0)), st_lo, st_hi],
        out_specs=(xspec, st_out, st_out),
        scratch_shapes=[pltpu.VMEM((nf, 4, fc), F32), pltpu.VMEM((2, tm + 8, fc // 2), F32)],
        compiler_params=_cparams(("parallel", "arbitrary", "arbitrary")), name="conv_ffn",
    )(h2, x2, w_up, w_up, w_conv, w_conv, b_conv, b_conv, w_down, conv_past, conv_past)
    if n_seq == 1:
        ca, cg = ca[:, -1], cg[:, -1]
    return x3, jnp.concatenate([ca, cg], axis=-1)


def _final_norm_kernel(x_ref, g_ref, o_ref):
    o_ref[...] = _rms_rows(x_ref[...], g_ref[...])


def _final_norm(x, g, *, tm):
    nb, s, d = x.shape
    spec = pl.BlockSpec((None, tm, d), lambda b, i: (b, i, 0))
    return pl.pallas_call(
        _final_norm_kernel, out_shape=jax.ShapeDtypeStruct(x.shape, F32), grid=(nb, s // tm),
        in_specs=[spec, pl.BlockSpec((1, d), lambda b, i: (0, 0))], out_specs=spec,
        compiler_params=_cparams(("parallel", "parallel")), name="final_norm",
    )(x, g)


def _rope_tables(pos):
    half = A_ROPE // 2
    inv = ROPE_THETA ** (-jnp.arange(half, dtype=F32) / half)
    ang = pos.astype(F32)[:, None] * inv[None, :]
    cos, sin = jnp.cos(ang), jnp.sin(ang)
    z = jnp.zeros((pos.shape[0], 128 - A_ROPE), F32)
    return jnp.concatenate([cos, cos, z], axis=1), jnp.concatenate([-sin, sin, z], axis=1)


def _chunk_major(w, fc):
    d, n = w.shape
    return w.reshape(d, n // fc, fc).transpose(1, 0, 2)


def _swap_halves_cols(w):
    half = w.shape[1] // 2
    return jnp.concatenate([w[:, half:], w[:, :half]], axis=1)


def _prep_layer(g_attn, w_in, g_q, w_uq, g_kv, w_uk, w_uv, rel_bias, g_heads, w_out, g_ffn, w_up, w_conv, b_conv,
                w_down):
    d = w_in.shape[0]
    kr = w_in[:, 1024:1024 + A_ROPE]
    z64 = jnp.zeros((d, 64), F32)
    w_ext = jnp.concatenate([w_in[:, :1024], kr, z64, _swap_halves_cols(kr), z64, w_in[:, 1024 + A_ROPE:]], axis=1)
    zq = jnp.zeros((w_uq.shape[0], 64), F32)
    main, swapped = [], []
    for hd in range(A_HEADS):
        lo = hd * (A_NOPE + A_ROPE)
        rope_cols = w_uq[:, lo + A_NOPE:lo + A_NOPE + A_ROPE]
        main += [w_uq[:, lo:lo + A_NOPE], rope_cols, zq]
        swapped += [_swap_halves_cols(rope_cols), zq]
    w_uq_ext = jnp.concatenate(main + swapped, axis=1)
    g_col = g_heads.reshape(-1, 1)
    return dict(
        g_attn=g_attn.reshape(1, -1), g_q=g_q.reshape(1, -1), g_kv=g_kv.reshape(1, -1),
        w_ext=w_ext.astype(BF16), w_uq_ext=w_uq_ext.astype(BF16), w_uk=w_uk.astype(BF16), w_uv=w_uv.astype(BF16),
        bias=_band_bias_table(rel_bias), g_a=g_col[0:512], g_b=g_col[512:1024], g_c=g_col[1024:1536],
        w_out=w_out.astype(BF16), g_ffn=g_ffn.reshape(1, -1), w_up=_chunk_major(w_up.astype(BF16), FFN_CHUNK), w_conv=w_conv,
        b_conv=b_conv.reshape(1, -1), w_down=w_down.astype(BF16))


def _tile(s, pref):
    t = min(s, pref)
    assert s % t == 0, (s, t)
    return t


def _layer_prompt(x, p, cos_t, sin_t):
    nb, s, d = x.shape
    tm = _tile(s, 256)
    tkc = tm
    (hq, ckv, ckvb, kr, krp, qb, kb, kbb, vb, vbt, qc, kc, kcp, vc, vctp) = _in_proj(
        x, cos_t, sin_t, p["g_attn"], p["g_q"], p["g_kv"], p["w_ext"], tm=tm, tkc=tkc)
    q_a = _q_proj_a(hq, cos_t, sin_t, p["w_uq_ext"], tm=_tile(s, 512))
    k_a, vt_a = _kv_proj_a(ckvb, krp, p["w_uk"], p["w_uv"], tm=_tile(s, 512))
    ta = _tile(s, 1024)
    mix_a = _attn_a(q_a, k_a, vt_a, p["g_a"], tq=ta, tk=ta // 2, n_past=0, nh=2)
    k_pad = jnp.pad(kbb, ((0, 0), (B_PAST, 0), (0, 0)))
    vt_pad = jnp.pad(vbt, ((0, 0), (0, 0), (B_PAST, 0)))
    mix_b = _attn_b(qb, k_pad, vt_pad, p["bias"], p["g_b"], hist0=0)
    mix_c = _attn_c(qc, kcp, vctp, p["g_c"], tq=_tile(s, 512), tk=tkc, n_past=0)
    x2, h2 = _out_proj(mix_a, mix_b, mix_c, x, p["w_out"], p["g_ffn"], tm=_tile(s, 512))
    conv0 = jnp.zeros((nb, CONV_W - 1, 2 * p["w_down"].shape[0]), F32)
    tf = _tile(s, 512)
    x3, conv_new = _ffn(h2, x2, p["w_up"], p["w_conv"], p["b_conv"], p["w_down"], conv0, tm=tf, seq=tf)
    rows = min(B_PAST, s)
    state = (ckv, kr, kb[:, s - rows:].reshape(nb, rows, B_HEADS, B_DIM), vb[:, s - rows:].reshape(nb, rows, B_HEADS, B_DIM),
             kc.reshape(nb, s, C_HEADS, C_DIM), vc.reshape(nb, s, C_HEADS, C_DIM), conv_new)
    return x3, state


def _pad_rows(t, n):
    return jnp.pad(t, ((0, 0), (0, n - t.shape[1]), (0, 0)))


def _layer_sample(x, p, cos_t, sin_t, ckv_past, kr_past, bk_past, bv_past, ck_past, cv_past, conv_past):
    nb, s, d = x.shape
    t = nb * s
    n_past = ckv_past.shape[1]
    xf = x.reshape(1, t, d)
    tm = _tile(t, 256)
    (hq, ckv, _, kr, krp, qb, kb, _, vb, _, qc, kc, _, vc, _) = _in_proj(
        xf, cos_t, sin_t, p["g_attn"], p["g_q"], p["g_kv"], p["w_ext"], tm=tm, tkc=tm)
    q_a = _q_proj_a(hq, cos_t, sin_t, p["w_uq_ext"], tm=tm)
    per = lambda a: a.reshape(nb, s, a.shape[-1])
    ckv, kr, kb, vb, kc, vc = per(ckv), per(kr), per(kb), per(vb), per(kc), per(vc)
    tqa = 256
    ska = n_past + tqa
    ckv_all = _pad_rows(jnp.concatenate([ckv_past, ckv], axis=1), ska).astype(BF16)
    kr_all = jnp.concatenate([kr_past, kr], axis=1)
    krp_all = _pad_rows(jnp.pad(kr_all, ((0, 0), (0, 0), (0, 128 - A_ROPE))), ska).astype(BF16)
    k_a, vt_a = _kv_proj_a(ckv_all, krp_all, p["w_uk"], p["w_uv"], tm=128)
    mix_a = _attn_a(_pad_rows(per(q_a), tqa), k_a, vt_a, p["g_a"], tq=tqa, tk=tqa // 2, n_past=n_past, nh=2)
    tq = 128
    sk = n_past + tq
    kb_all = _pad_rows(jnp.concatenate([bk_past.reshape(nb, -1, 512), kb], axis=1), B_WIN).astype(BF16)
    vb_all = _pad_rows(jnp.concatenate([bv_past.reshape(nb, -1, 512), vb], axis=1), B_WIN).astype(BF16)
    mix_b = _attn_b(_pad_rows(per(qb), B_QTILE), kb_all, vb_all.transpose(0, 2, 1), p["bias"], p["g_b"],
                    hist0=bk_past.shape[1])
    kc_all = _pad_rows(jnp.concatenate([ck_past.reshape(nb, -1, 512), kc], axis=1), sk)
    vc_all = _pad_rows(jnp.concatenate([cv_past.reshape(nb, -1, 512), vc], axis=1), sk)
    kcp = _permute_keys(kc_all, 128).astype(BF16)
    vctp = _permute_keys(vc_all, 128).astype(BF16).transpose(0, 2, 1)
    mix_c = _attn_c(_pad_rows(per(qc), tq), kcp, vctp, p["g_c"], tq=tq, tk=128, n_past=n_past)
    flat = lambda mx: mx[:, :, :s].transpose(1, 0, 2).reshape(1, 512, t)
    x2, h2 = _out_proj(flat(mix_a), flat(mix_b), flat(mix_c), xf, p["w_out"], p["g_ffn"], tm=tm)
    x3, conv_new = _ffn(h2, x2, p["w_up"], p["w_conv"], p["b_conv"], p["w_down"], conv_past, tm=t, seq=s)
    x3 = x3.reshape(nb, s, d)
    state = (ckv, kr, kb.reshape(nb, s, B_HEADS, B_DIM), vb.reshape(nb, s, B_HEADS, B_DIM),
             kc.reshape(nb, s, C_HEADS, C_DIM), vc.reshape(nb, s, C_HEADS, C_DIM), conv_new)
    return x3, state


def kernel(x_prompt, x_sample, cache_a_ckv, cache_a_krope, cache_b_k, cache_b_v, cache_c_k, cache_c_v, state_conv,
           g_attn, w_in, g_q, w_uq, g_kv, w_uk, w_uv, rel_bias, g_heads, w_out, g_ffn, w_up, w_conv, b_conv, w_down,
           g_final):
    depth = w_in.shape[0]
    n_tp = x_prompt.shape[1]
    nbs, n_ts, _ = x_sample.shape
    past_len = cache_c_k.shape[2]
    cos_p, sin_p = _rope_tables(jnp.arange(n_tp, dtype=jnp.int32))
    pos_s = jnp.tile(past_len + jnp.arange(n_ts, dtype=jnp.int32), nbs)
    cos_s, sin_s = _rope_tables(pos_s)

    xp, xs = x_prompt, x_sample
    p_layers, s_layers = [], []
    for l in range(depth):
        p = _prep_layer(g_attn[l], w_in[l], g_q[l], w_uq[l], g_kv[l], w_uk[l], w_uv[l], rel_bias[l], g_heads[l],
                        w_out[l], g_ffn[l], w_up[l], w_conv[l], b_conv[l], w_down[l])
        xp, p_st = _layer_prompt(xp, p, cos_p, sin_p)
        xs, s_st = _layer_sample(xs, p, cos_s, sin_s, cache_a_ckv[l], cache_a_krope[l], cache_b_k[l], cache_b_v[l],
                                 cache_c_k[l], cache_c_v[l], state_conv[l])
        p_layers.append(p_st)
        s_layers.append(s_st)

    gf = g_final.reshape(1, -1)
    y_prompt = _final_norm(xp, gf, tm=_tile(n_tp, 512))
    y_sample = _final_norm(xs, gf, tm=n_ts)
    p_out = [jnp.stack(t) for t in zip(*p_layers)]
    s_out = [jnp.stack(t) for t in zip(*s_layers)]
    return (y_prompt, y_sample, *p_out, *s_out)
```

```python
import functools
import math

import numpy as np
import jax
import jax.numpy as jnp
from jax import lax
from jax.experimental import pallas as pl
from jax.experimental.pallas import tpu as pltpu

F32 = jnp.float32
BF16 = jnp.bfloat16

EPS = 1e-6
CHUNK = 64
ROPE_THETA = 10000.0
A_HEADS, A_NOPE, A_ROPE, A_VDIM = 4, 128, 64, 128
A_QK_PAD = 256
B_HEADS, B_DIM, B_PREV_CHUNKS, B_REL_CLIP = 4, 128, 8, 128
B_PAST = B_PREV_CHUNKS * CHUNK
C_HEADS, C_DIM = 4, 128
HEAD_DIM = 128
CONV_W = 3
LOG2E = 1.4426950408889634
A_SCALE = 1.0 / math.sqrt(A_NOPE + A_ROPE)
B_SCALE = 1.0 / math.sqrt(B_DIM)
C_SCALE = 1.0 / math.sqrt(C_DIM)

SB_EXIT_LOG2 = 160.0

B_QTILE = 4 * CHUNK
B_WIN = (B_PREV_CHUNKS + 4) * CHUNK

FFN_CHUNK = 512

VMEM_LIMIT = 56 * 1024 * 1024


def _cparams(sem):
    return pltpu.CompilerParams(dimension_semantics=sem, vmem_limit_bytes=VMEM_LIMIT)


def _rms_rows(x, g):
    ms = jnp.mean(x * x, axis=-1, keepdims=True)
    return x * lax.rsqrt(ms + EPS) * g


def _rms_cols(o, g):
    ms = jnp.mean(o * o, axis=0, keepdims=True)
    return o * lax.rsqrt(ms + EPS) * g


_NT = (((1,), (1,)), ((), ()))
_TN = (((0,), (0,)), ((), ()))


def _resident(shape, index_map):
    return pl.BlockSpec(shape, index_map, pipeline_mode=pl.Buffered(1))


IN_COLS = 512 + 512 + 256 + 1536 + 1536


def _in_proj_kernel(x_ref, cos_ref, sin_ref, gattn_ref, gq_ref, gkv_ref, w_ref,
                    hq_ref, ckv_ref, ckvb_ref, kr_ref, krp_ref,
                    qb_ref, kb_ref, kbb_ref, vb_ref, vbt_ref,
                    qc_ref, kc_ref, kcp_ref, vc_ref, vctp_ref, ks_ref, vs_ref, *, tkc):
    h = _rms_rows(x_ref[...], gattn_ref[...]).astype(BF16)

    def proj(lo, n):
        return jnp.dot(h, w_ref[:, lo:lo + n], preferred_element_type=F32)

    hq_ref[...] = _rms_rows(proj(0, 512), gq_ref[...]).astype(BF16)
    ckv = _rms_rows(proj(512, 512), gkv_ref[...])
    ckv_ref[...] = ckv
    ckvb_ref[...] = ckv.astype(BF16)
    kr2 = proj(1024, 256)
    krp = kr2[:, :128] * cos_ref[...] + kr2[:, 128:] * sin_ref[...]
    krp_ref[...] = krp.astype(BF16)
    kr_ref[...] = krp[:, :A_ROPE]
    o = 1280
    qb_ref[...] = (proj(o, 512) * (B_SCALE * LOG2E)).astype(BF16)
    kb = proj(o + 512, 512)
    kb_ref[...] = kb
    kbb_ref[...] = kb.astype(BF16)
    vb = proj(o + 1024, 512)
    vb_ref[...] = vb
    vbt_ref[...] = vb.T.astype(BF16)
    o = 2816
    qc_ref[...] = (proj(o, 512) * (C_SCALE * LOG2E)).astype(BF16)
    kc = proj(o + 512, 512)
    vc = proj(o + 1024, 512)
    kc_ref[...] = kc
    vc_ref[...] = vc
    tm = kc_ref.shape[0]
    m = tkc // 8
    for hd in range(C_HEADS):
        cols = slice(hd * C_DIM, (hd + 1) * C_DIM)
        ks_ref[hd] = kc[:, cols]
        vs_ref[hd] = vc[:, cols]
        for t0 in range(0, tm, tkc):
            kp = jnp.concatenate([ks_ref[hd, pl.ds(t0 + i, 8, stride=m), :] for i in range(m)], axis=0)
            kcp_ref[t0:t0 + tkc, cols] = kp.astype(BF16)
            vp = jnp.concatenate([vs_ref[hd, pl.ds(t0 + i, 8, stride=m), :] for i in range(m)], axis=0)
            vctp_ref[cols, t0:t0 + tkc] = vp.T.astype(BF16)


def _in_proj(x, cos_t, sin_t, g_attn, g_q, g_kv, w_ext, *, tm, tkc):
    nb, s, d = x.shape
    grid = (nb, s // tm)
    tok = lambda c, dt: jax.ShapeDtypeStruct((nb, s, c), dt)
    tr = lambda c: jax.ShapeDtypeStruct((nb, c, s), BF16)
    out_shape = (tok(512, BF16), tok(512, F32), tok(512, BF16), tok(A_ROPE, F32), tok(128, BF16),
                 tok(512, BF16), tok(512, F32), tok(512, BF16), tok(512, F32), tr(512),
                 tok(512, BF16), tok(512, F32), tok(512, BF16), tok(512, F32), tr(512))
    tspec = lambda c: pl.BlockSpec((None, tm, c), lambda b, i: (b, i, 0))
    trspec = pl.BlockSpec((None, 512, tm), lambda b, i: (b, 0, i))
    out_specs = (tspec(512), tspec(512), tspec(512), tspec(A_ROPE), tspec(128),
                 tspec(512), tspec(512), tspec(512), tspec(512), trspec,
                 tspec(512), tspec(512), tspec(512), tspec(512), trspec)
    vec = lambda n: _resident((1, n), lambda b, i: (0, 0))
    in_specs = [tspec(d),
                pl.BlockSpec((tm, 128), lambda b, i: (i, 0)),
                pl.BlockSpec((tm, 128), lambda b, i: (i, 0)),
                vec(d), vec(512), vec(512),
                _resident((d, IN_COLS), lambda b, i: (0, 0))]
    return pl.pallas_call(
        functools.partial(_in_proj_kernel, tkc=tkc),
        out_shape=out_shape, grid=grid, in_specs=in_specs, out_specs=out_specs,
        scratch_shapes=[pltpu.VMEM((C_HEADS, tm, C_DIM), F32), pltpu.VMEM((C_HEADS, tm, C_DIM), F32)],
        compiler_params=_cparams(("parallel", "parallel")), name="in_proj",
    )(x, cos_t, sin_t, g_attn, g_q, g_kv, w_ext)


def _q_proj_a_kernel(hq_ref, cos_ref, sin_ref, w_ref, q_ref):
    ql = jnp.dot(hq_ref[...], w_ref[...], preferred_element_type=F32)
    cos, sin = cos_ref[...], sin_ref[...]
    sc = A_SCALE * LOG2E
    for hd in range(A_HEADS):
        lo = hd * A_QK_PAD
        rot = ql[:, lo + 128:lo + 256] * cos + ql[:, 1024 + hd * 128:1024 + (hd + 1) * 128] * sin
        q_ref[:, lo:lo + 128] = (ql[:, lo:lo + 128] * sc).astype(BF16)
        q_ref[:, lo + 128:lo + 256] = (rot * sc).astype(BF16)


def _q_proj_a(hq, cos_t, sin_t, w_uq_ext, *, tm):
    nb, s, _ = hq.shape
    return pl.pallas_call(
        _q_proj_a_kernel,
        out_shape=jax.ShapeDtypeStruct((nb, s, A_HEADS * A_QK_PAD), BF16),
        grid=(nb, s // tm),
        in_specs=[pl.BlockSpec((None, tm, 512), lambda b, i: (b, i, 0)),
                  pl.BlockSpec((tm, 128), lambda b, i: (i, 0)),
                  pl.BlockSpec((tm, 128), lambda b, i: (i, 0)),
                  _resident((512, 1536), lambda b, i: (0, 0))],
        out_specs=pl.BlockSpec((None, tm, A_HEADS * A_QK_PAD), lambda b, i: (b, i, 0)),
        compiler_params=_cparams(("parallel", "parallel")), name="q_proj_a",
    )(hq, cos_t, sin_t, w_uq_ext)


def _kv_proj_a_kernel(ckv_ref, krp_ref, wk_ref, wv_ref, k_ref, vt_ref):
    c = ckv_ref[...]
    kn = jnp.dot(c, wk_ref[...], preferred_element_type=F32)
    v = jnp.dot(c, wv_ref[...], preferred_element_type=F32)
    krp = krp_ref[...]
    for hd in range(A_HEADS):
        lo = hd * A_QK_PAD
        k_ref[:, lo:lo + 128] = kn[:, hd * 128:(hd + 1) * 128].astype(BF16)
        k_ref[:, lo + 128:lo + 256] = krp
    vt_ref[...] = v.T.astype(BF16)


def _kv_proj_a(ckvb, krp, w_uk, w_uv, *, tm):
    nb, s, _ = ckvb.shape
    return pl.pallas_call(
        _kv_proj_a_kernel,
        out_shape=(jax.ShapeDtypeStruct((nb, s, A_HEADS * A_QK_PAD), BF16),
                   jax.ShapeDtypeStruct((nb, A_HEADS * A_VDIM, s), BF16)),
        grid=(nb, s // tm),
        in_specs=[pl.BlockSpec((None, tm, 512), lambda b, i: (b, i, 0)),
                  pl.BlockSpec((None, tm, 128), lambda b, i: (b, i, 0)),
                  _resident((512, 512), lambda b, i: (0, 0)),
                  _resident((512, 512), lambda b, i: (0, 0))],
        out_specs=(pl.BlockSpec((None, tm, A_HEADS * A_QK_PAD), lambda b, i: (b, i, 0)),
                   pl.BlockSpec((None, A_HEADS * A_VDIM, tm), lambda b, i: (b, 0, i))),
        compiler_params=_cparams(("parallel", "parallel")), name="kv_proj_a",
    )(ckvb, krp, w_uk, w_uv)


def _attn_a_kernel(q_ref, k_ref, vt_ref, g_ref, o_ref, s0_ref, s1_ref, m_ref, l_ref, acc_ref, *, tq, tk, n_past, nh):
    qi = pl.program_id(2)
    q0 = n_past + qi * tq
    n_full = q0 // tk
    m_ref[...] = jnp.full(m_ref.shape, -jnp.inf, F32)
    l_ref[...] = jnp.zeros(l_ref.shape, F32)
    acc_ref[...] = jnp.zeros(acc_ref.shape, F32)

    def scores(j, s_ref):
        start = pl.multiple_of(j * tk, tk)
        for hd in range(nh):
            qk = slice(hd * A_QK_PAD, (hd + 1) * A_QK_PAD)
            s_ref[hd] = lax.dot_general(k_ref[pl.ds(start, tk), qk], q_ref[:, qk], _NT, preferred_element_type=F32)

    def softmax_pv(j, s_ref, masked):
        start = pl.multiple_of(j * tk, tk)
        if masked:
            kpos = start + lax.broadcasted_iota(jnp.int32, (tk, tq), 0)
            qpos = q0 + lax.broadcasted_iota(jnp.int32, (tk, tq), 1)
            visible = (kpos >> 6) <= (qpos >> 6)
        for hd in range(nh):
            vv = slice(hd * A_VDIM, (hd + 1) * A_VDIM)
            s = s_ref[hd]
            if masked:
                s = jnp.where(visible, s, -jnp.inf)
            m_old = m_ref[hd]
            m_new = jnp.maximum(m_old, jnp.max(s, axis=0, keepdims=True))
            p = jnp.exp2(s - m_new)
            alpha = jnp.exp2(m_old - m_new)
            l_ref[hd] = alpha * l_ref[hd] + jnp.sum(p, axis=0, keepdims=True)
            pv = jnp.dot(vt_ref[vv, pl.ds(start, tk)], p.astype(BF16), preferred_element_type=F32)
            acc_ref[vv, :] = alpha * acc_ref[vv, :] + pv
            m_ref[hd] = m_new

    def pair_body(t, c):
        j = 2 * t
        scores(j + 1, s1_ref)
        softmax_pv(j, s0_ref, False)
        scores(j + 2, s0_ref)
        softmax_pv(j + 1, s1_ref, False)
        return c

    scores(0, s0_ref)
    lax.fori_loop(0, n_full // 2, pair_body, 0)
    scores(n_full + 1, s1_ref)
    softmax_pv(n_full, s0_ref, True)
    softmax_pv(n_full + 1, s1_ref, True)
    for hd in range(nh):
        vv = slice(hd * A_VDIM, (hd + 1) * A_VDIM)
        o = acc_ref[vv, :] / l_ref[hd]
        o_ref[vv, :] = _rms_cols(o, g_ref[vv, :]).astype(BF16)


def _attn_a(q, k, vt, g_col, *, tq, tk, n_past, nh):
    nb, sq, _ = q.shape
    sk = k.shape[1]
    assert tq == 2 * tk and n_past % tq == 0 and sq % tq == 0 and sk >= n_past + sq
    return pl.pallas_call(
        functools.partial(_attn_a_kernel, tq=tq, tk=tk, n_past=n_past, nh=nh),
        out_shape=jax.ShapeDtypeStruct((nb, A_HEADS * A_VDIM, sq), BF16),
        grid=(nb, A_HEADS // nh, sq // tq),
        in_specs=[pl.BlockSpec((None, tq, nh * A_QK_PAD), lambda b, h, i: (b, i, h)),
                  _resident((None, sk, nh * A_QK_PAD), lambda b, h, i: (b, 0, h)),
                  _resident((None, nh * A_VDIM, sk), lambda b, h, i: (b, h, 0)),
                  pl.BlockSpec((nh * A_VDIM, 1), lambda b, h, i: (h, 0))],
        out_specs=pl.BlockSpec((None, nh * A_VDIM, tq), lambda b, h, i: (b, h, i)),
        scratch_shapes=[pltpu.VMEM((nh, tk, tq), F32), pltpu.VMEM((nh, tk, tq), F32),
                        pltpu.VMEM((nh, 1, tq), F32), pltpu.VMEM((nh, 1, tq), F32),
                        pltpu.VMEM((nh * A_VDIM, tq), F32)],
        compiler_params=_cparams(("parallel", "parallel", "arbitrary")), name="attn_a",
    )(q, k, vt, g_col)


def _attn_b_kernel(q_ref, k0_ref, k1_ref, k2_ref, v0_ref, v1_ref, v2_ref, bias_ref, g_ref, o_ref, *, hist0):
    st = pl.program_id(1)
    thr = B_PAST - hist0 - st * B_QTILE
    row_ok = lax.broadcasted_iota(jnp.int32, (B_WIN, B_QTILE), 0) >= thr
    for hd in range(B_HEADS):
        cols = slice(hd * B_DIM, (hd + 1) * B_DIM)
        k = jnp.concatenate([k0_ref[:, cols], k1_ref[:, cols], k2_ref[:, cols]], axis=0)
        s = lax.dot_general(k, q_ref[:, cols], _NT, preferred_element_type=F32) + bias_ref[hd]
        s = jnp.where(row_ok, s, -jnp.inf)
        m = jnp.max(s, axis=0, keepdims=True)
        p = jnp.exp2(s - m)
        l = jnp.sum(p, axis=0, keepdims=True)
        v = jnp.concatenate([v0_ref[cols, :], v1_ref[cols, :], v2_ref[cols, :]], axis=1)
        o = jnp.dot(v, p.astype(BF16), preferred_element_type=F32) / l
        o_ref[cols, :] = _rms_cols(o, g_ref[cols, :]).astype(BF16)


def _attn_b(q, k_pad, vt_pad, bias, g_col, *, hist0, lead):
    nb, sq, w = q.shape
    kspec = lambda o: pl.BlockSpec((None, B_QTILE, w), lambda b, i: (b, jnp.maximum(i + o - lead, 0), 0))
    vspec = lambda o: pl.BlockSpec((None, w, B_QTILE), lambda b, i: (b, 0, jnp.maximum(i + o - lead, 0)))
    return pl.pallas_call(
        functools.partial(_attn_b_kernel, hist0=hist0),
        out_shape=jax.ShapeDtypeStruct((nb, w, sq), BF16),
        grid=(nb, sq // B_QTILE),
        in_specs=[pl.BlockSpec((None, B_QTILE, w), lambda b, i: (b, i, 0)),
                  kspec(0), kspec(1), kspec(2), vspec(0), vspec(1), vspec(2),
                  pl.BlockSpec((B_HEADS, B_WIN, B_QTILE), lambda b, i: (0, 0, 0)),
                  pl.BlockSpec((w, 1), lambda b, i: (0, 0))],
        out_specs=pl.BlockSpec((None, w, B_QTILE), lambda b, i: (b, 0, i)),
        compiler_params=_cparams(("parallel", "arbitrary")), name="attn_b",
    )(q, k_pad, k_pad, k_pad, vt_pad, vt_pad, vt_pad, bias, g_col)


def _band_bias_table(rel_bias):
    nh = rel_bias.shape[0]
    ring = 1024
    n_lo = B_WIN - 1 - (B_PAST + B_REL_CLIP)
    n_hi = B_PAST - B_REL_CLIP - 1
    lo = jnp.broadcast_to(rel_bias[:, :1], (nh, n_lo))
    hi = lambda n: jnp.broadcast_to(rel_bias[:, -1:], (nh, n))
    gap = ring - (B_QTILE + n_lo + rel_bias.shape[1] + n_hi)
    v = jnp.concatenate([hi(B_QTILE), jnp.zeros((nh, gap), rel_bias.dtype), lo, rel_bias, hi(n_hi)], axis=1)
    tbl = jnp.tile(v, (1, B_WIN))[:, :B_WIN * (ring - 1)].reshape(nh, B_WIN, ring - 1)[:, :, :B_QTILE] * LOG2E
    kc = np.arange(B_WIN)[:, None] // CHUNK
    qc = np.arange(B_QTILE)[None, :] // CHUNK
    valid = (kc >= qc) & (kc <= qc + B_PREV_CHUNKS)
    return jnp.where(valid[None], tbl, -jnp.inf)


def _attn_c_kernel(q_ref, kp_ref, vtp_ref, g_ref, o_ref, c_ref, acc_ref, *, tq, tk, n_past, nh):
    qi = pl.program_id(1)
    q0 = n_past + qi * tq
    n_full = q0 // tk
    n_mask = tq // tk
    m = tk // 8
    c_ref[...] = jnp.zeros(c_ref.shape, F32)
    acc_ref[...] = jnp.zeros(acc_ref.shape, F32)
    sub = lax.broadcasted_iota(jnp.int32, (8, tq), 0)

    def tile(j, masked):
        start = pl.multiple_of(j * tk, tk)
        zs = []
        for hd in range(nh):
            cols = slice(hd * C_DIM, (hd + 1) * C_DIM)
            zs.append(lax.dot_general(kp_ref[pl.ds(start, tk), cols], q_ref[:, cols], _NT,
                                      preferred_element_type=F32))
        if masked:
            r = lax.broadcasted_iota(jnp.int32, (tk, tq), 0)
            kpos = start + (r & 7) * m + (r >> 3)
            qpos = q0 + lax.broadcasted_iota(jnp.int32, (tk, tq), 1)
            causal = kpos < qpos
        cmin = None
        for hd in range(nh):
            cols = slice(hd * C_DIM, (hd + 1) * C_DIM)
            z = zs[hd]
            sp = jnp.maximum(z, 0.0) + jnp.log2(1.0 + jnp.exp2(-jnp.abs(z)))
            if masked:
                sp = jnp.where(causal, sp, 0.0)
            run = [None] * m
            run[m - 1] = sp[(m - 1) * 8:m * 8]
            for i in range(m - 2, -1, -1):
                run[i] = sp[i * 8:(i + 1) * 8] + run[i + 1]
            tot = run[0]
            incl = tot
            for sh in (1, 2, 4):
                incl = incl + jnp.where(sub + sh < 8, pltpu.roll(incl, 8 - sh, 0), 0.0)
            c_in = c_ref[hd]
            off = (incl - tot) + c_in
            rows = []
            for i in range(m):
                w = jnp.exp2(z[i * 8:(i + 1) * 8] - (run[i] + off))
                if masked:
                    w = jnp.where(causal[i * 8:(i + 1) * 8], w, 0.0)
                rows.append(w)
            w = jnp.concatenate(rows, axis=0).astype(BF16)
            acc_ref[cols, :] += jnp.dot(vtp_ref[cols, pl.ds(start, tk)], w, preferred_element_type=F32)
            c_new = c_in + incl[0:1, :]
            c_ref[hd] = c_new
            hmin = jnp.min(c_new)
            cmin = hmin if cmin is None else jnp.minimum(cmin, hmin)
        return cmin

    cmin = jnp.float32(0.0)
    for jj in range(n_mask - 1, -1, -1):
        cmin = tile(n_full + jj, True)

    def cond(carry):
        j, cm = carry
        return jnp.logical_and(j >= 0, cm < SB_EXIT_LOG2)

    def body(carry):
        j, _ = carry
        return j - 1, tile(j, False)

    lax.while_loop(cond, body, (n_full - 1, cmin))
    for hd in range(nh):
        cols = slice(hd * C_DIM, (hd + 1) * C_DIM)
        o_ref[cols, :] = _rms_cols(acc_ref[cols, :], g_ref[cols, :]).astype(BF16)


def _attn_c(q, kp, vtp, g_col, *, tq, tk, n_past):
    nb, sq, w = q.shape
    sk = kp.shape[1]
    nh = w // C_DIM
    return pl.pallas_call(
        functools.partial(_attn_c_kernel, tq=tq, tk=tk, n_past=n_past, nh=nh),
        out_shape=jax.ShapeDtypeStruct((nb, w, sq), BF16),
        grid=(nb, sq // tq),
        in_specs=[pl.BlockSpec((None, tq, w), lambda b, i: (b, i, 0)),
                  _resident((None, sk, w), lambda b, i: (b, 0, 0)),
                  _resident((None, w, sk), lambda b, i: (b, 0, 0)),
                  pl.BlockSpec((w, 1), lambda b, i: (0, 0))],
        out_specs=pl.BlockSpec((None, w, tq), lambda b, i: (b, 0, i)),
        scratch_shapes=[pltpu.VMEM((nh, 1, tq), F32), pltpu.VMEM((w, tq), F32)],
        compiler_params=_cparams(("parallel", "arbitrary")), name="attn_c",
    )(q, kp, vtp, g_col)


def _permute_keys(t, tk):
    nb, s, c = t.shape
    m = tk // 8
    return t.reshape(nb, s // tk, 8, m, c).transpose(0, 1, 3, 2, 4).reshape(nb, s, c)


def _out_proj_kernel(ma_ref, mb_ref, mc_ref, x_ref, w_ref, g_ref, x2_ref, h2_ref):
    y = lax.dot_general(ma_ref[...], w_ref[0:512, :], _TN, preferred_element_type=F32)
    y = y + lax.dot_general(mb_ref[...], w_ref[512:1024, :], _TN, preferred_element_type=F32)
    y = y + lax.dot_general(mc_ref[...], w_ref[1024:1536, :], _TN, preferred_element_type=F32)
    x2 = x_ref[...] + y
    x2_ref[...] = x2
    h2_ref[...] = _rms_rows(x2, g_ref[...]).astype(BF16)


def _out_proj(ma, mb, mc, x, w_out, g_ffn, *, tm):
    nb, s, d = x.shape
    mspec = pl.BlockSpec((None, 512, tm), lambda b, i: (b, 0, i))
    xspec = pl.BlockSpec((None, tm, d), lambda b, i: (b, i, 0))
    return pl.pallas_call(
        _out_proj_kernel,
        out_shape=(jax.ShapeDtypeStruct((nb, s, d), F32), jax.ShapeDtypeStruct((nb, s, d), BF16)),
        grid=(nb, s // tm),
        in_specs=[mspec, mspec, mspec, xspec,
                  _resident((1536, d), lambda b, i: (0, 0)),
                  _resident((1, d), lambda b, i: (0, 0))],
        out_specs=(xspec, xspec),
        compiler_params=_cparams(("parallel", "parallel")), name="out_proj",
    )(ma, mb, mc, x, w_out, g_ffn)


def _ffn_kernel(h_ref, x_ref, wa_ref, wg_ref, wca_ref, wcg_ref, ba_ref, bg_ref, wd_ref, pa_ref, pg_ref,
                o_ref, ca_ref, cg_ref, carry_ref, us_ref, *, seq, n_seq, fc):
    i = pl.program_id(1)
    f = pl.program_id(2)
    tm = n_seq * seq

    @pl.when(f == 0)
    def _():
        o_ref[...] = x_ref[...]

    if n_seq == 1:
        @pl.when(i == 0)
        def _():
            carry_ref[f, 0:2, :] = pa_ref[...]
            carry_ref[f, 2:4, :] = pg_ref[...]

    h = h_ref[...]
    hw = fc // 2
    r8 = lax.broadcasted_iota(jnp.int32, (8, hw), 0)

    def conv(slot, u, prev, wc, b):
        us_ref[slot, 6:8, :] = prev[0]
        us_ref[slot, 8:8 + tm, :] = u
        u1 = us_ref[slot, pl.ds(7, tm), :]
        u2 = us_ref[slot, pl.ds(6, tm), :]
        if n_seq > 1:
            p1, p2 = [], []
            for sq in range(n_seq):
                lo = sq * seq
                pr = prev[sq]
                p1 += [jnp.where(r8 == 0, pr[1:2, :], u1[lo:lo + 8]), u1[lo + 8:lo + seq]]
                p2 += [jnp.where(r8 == 0, pr[0:1, :], jnp.where(r8 == 1, pr[1:2, :], u2[lo:lo + 8])),
                       u2[lo + 8:lo + seq]]
            u1 = jnp.concatenate(p1, axis=0)
            u2 = jnp.concatenate(p2, axis=0)
        return b + wc[0:1, :] * u2 + wc[1:2, :] * u1 + wc[2:3, :] * u

    acts = []
    for half in range(2):
        cols = slice(half * hw, (half + 1) * hw)
        ua = jnp.dot(h, wa_ref[:, cols], preferred_element_type=F32)
        ug = jnp.dot(h, wg_ref[:, cols], preferred_element_type=F32)
        if n_seq == 1:
            prev_a = [carry_ref[f, 0:2, cols]]
            prev_g = [carry_ref[f, 2:4, cols]]
        else:
            prev_a = [pa_ref[sq, :, cols] for sq in range(n_seq)]
            prev_g = [pg_ref[sq, :, cols] for sq in range(n_seq)]
        a = conv(0, ua, prev_a, wca_ref[:, cols], ba_ref[:, cols])
        g = conv(1, ug, prev_g, wcg_ref[:, cols], bg_ref[:, cols])
        acts.append((a * (1.0 / (1.0 + jnp.exp(-a))) * g).astype(BF16))
        for sq in range(n_seq):
            hi = (sq + 1) * seq
            if n_seq == 1:
                carry_ref[f, 0:2, cols] = ua[hi - 2:hi, :]
                carry_ref[f, 2:4, cols] = ug[hi - 2:hi, :]
                ca_ref[:, cols] = ua[hi - 2:hi, :]
                cg_ref[:, cols] = ug[hi - 2:hi, :]
            else:
                ca_ref[sq, :, cols] = ua[hi - 2:hi, :]
                cg_ref[sq, :, cols] = ug[hi - 2:hi, :]
    o_ref[...] += jnp.dot(jnp.concatenate(acts, axis=1), wd_ref[...], preferred_element_type=F32)


def _ffn(h2, x2, w_up, w_conv, b_conv, w_down, conv_past, *, tm, seq):
    nb, s, d = x2.shape
    dff = w_down.shape[0]
    fc = w_up.shape[2]
    nf = dff // fc
    n_seq = tm // seq
    assert (n_seq == 1 or (nb == 1 and tm == s)) and fc % 256 == 0 and seq % 8 == 0 and seq > 8
    xspec = pl.BlockSpec((None, tm, d), lambda b, i, f: (b, i, 0))
    xin = _resident((None, tm, d), lambda b, i, f: (b, i, 0))
    lo = lambda r: pl.BlockSpec((r, fc), lambda b, i, f: (0, f))
    hi = lambda r: pl.BlockSpec((r, fc), lambda b, i, f: (0, nf + f))
    up_lo = pl.BlockSpec((None, d, fc), lambda b, i, f: (f, 0, 0))
    up_hi = pl.BlockSpec((None, d, fc), lambda b, i, f: (nf + f, 0, 0))
    if n_seq == 1:
        st_lo = pl.BlockSpec((None, CONV_W - 1, fc), lambda b, i, f: (b, 0, f))
        st_hi = pl.BlockSpec((None, CONV_W - 1, fc), lambda b, i, f: (b, 0, nf + f))
        st_out = pl.BlockSpec((None, None, CONV_W - 1, fc), lambda b, i, f: (b, i, 0, f))
        st_shape = jax.ShapeDtypeStruct((nb, s // tm, CONV_W - 1, dff), F32)
    else:
        st_lo = pl.BlockSpec((n_seq, CONV_W - 1, fc), lambda b, i, f: (0, 0, f))
        st_hi = pl.BlockSpec((n_seq, CONV_W - 1, fc), lambda b, i, f: (0, 0, nf + f))
        st_out = st_lo
        st_shape = jax.ShapeDtypeStruct((n_seq, CONV_W - 1, dff), F32)
    x3, ca, cg = pl.pallas_call(
        functools.partial(_ffn_kernel, seq=seq, n_seq=n_seq, fc=fc),
        out_shape=(jax.ShapeDtypeStruct((nb, s, d), F32), st_shape, st_shape),
        grid=(nb, s // tm, nf),
        in_specs=[xspec, xin, up_lo, up_hi, lo(CONV_W), hi(CONV_W), lo(1), hi(1),
                  pl.BlockSpec((fc, d), lambda b, i, f: (f, 0)), st_lo, st_hi],
        out_specs=(xspec, st_out, st_out),
        scratch_shapes=[pltpu.VMEM((nf, 4, fc), F32), pltpu.VMEM((2, tm + 8, fc // 2), F32)],
        compiler_params=_cparams(("parallel", "arbitrary", "arbitrary")), name="conv_ffn",
    )(h2, x2, w_up, w_up, w_conv, w_conv, b_conv, b_conv, w_down, conv_past, conv_past)
    if n_seq == 1:
        ca, cg = ca[:, -1], cg[:, -1]
    return x3, jnp.concatenate([ca, cg], axis=-1)


def _final_norm_kernel(x_ref, g_ref, o_ref):
    o_ref[...] = _rms_rows(x_ref[...], g_ref[...])


def _final_norm(x, g, *, tm):
    nb, s, d = x.shape
    spec = pl.BlockSpec((None, tm, d), lambda b, i: (b, i, 0))
    return pl.pallas_call(
        _final_norm_kernel, out_shape=jax.ShapeDtypeStruct(x.shape, F32), grid=(nb, s // tm),
        in_specs=[spec, pl.BlockSpec((1, d), lambda b, i: (0, 0))], out_specs=spec,
        compiler_params=_cparams(("parallel", "parallel")), name="final_norm",
    )(x, g)


def _rope_tables(pos):
    half = A_ROPE // 2
    inv = ROPE_THETA ** (-jnp.arange(half, dtype=F32) / half)
    ang = pos.astype(F32)[:, None] * inv[None, :]
    cos, sin = jnp.cos(ang), jnp.sin(ang)
    z = jnp.zeros((pos.shape[0], 128 - A_ROPE), F32)
    return jnp.concatenate([cos, cos, z], axis=1), jnp.concatenate([-sin, sin, z], axis=1)


def _chunk_major(w, fc):
    d, n = w.shape
    return w.reshape(d, n // fc, fc).transpose(1, 0, 2)


def _swap_halves_cols(w):
    half = w.shape[1] // 2
    return jnp.concatenate([w[:, half:], w[:, :half]], axis=1)


def _prep_layer(g_attn, w_in, g_q, w_uq, g_kv, w_uk, w_uv, rel_bias, g_heads, w_out, g_ffn, w_up, w_conv, b_conv,
                w_down):
    d = w_in.shape[0]
    kr = w_in[:, 1024:1024 + A_ROPE]
    z64 = jnp.zeros((d, 64), F32)
    w_ext = jnp.concatenate([w_in[:, :1024], kr, z64, _swap_halves_cols(kr), z64, w_in[:, 1024 + A_ROPE:]], axis=1)
    zq = jnp.zeros((w_uq.shape[0], 64), F32)
    main, swapped = [], []
    for hd in range(A_HEADS):
        lo = hd * (A_NOPE + A_ROPE)
        rope_cols = w_uq[:, lo + A_NOPE:lo + A_NOPE + A_ROPE]
        main += [w_uq[:, lo:lo + A_NOPE], rope_cols, zq]
        swapped += [_swap_halves_cols(rope_cols), zq]
    w_uq_ext = jnp.concatenate(main + swapped, axis=1)
    g_col = g_heads.reshape(-1, 1)
    return dict(
        g_attn=g_attn.reshape(1, -1), g_q=g_q.reshape(1, -1), g_kv=g_kv.reshape(1, -1),
        w_ext=w_ext.astype(BF16), w_uq_ext=w_uq_ext.astype(BF16), w_uk=w_uk.astype(BF16), w_uv=w_uv.astype(BF16),
        bias=_band_bias_table(rel_bias), g_a=g_col[0:512], g_b=g_col[512:1024], g_c=g_col[1024:1536],
        w_out=w_out.astype(BF16), g_ffn=g_ffn.reshape(1, -1), w_up=_chunk_major(w_up.astype(BF16), FFN_CHUNK), w_conv=w_conv,
        b_conv=b_conv.reshape(1, -1), w_down=w_down.astype(BF16))


def _tile(s, pref):
    t = min(s, pref)
    assert s % t == 0, (s, t)
    return t


def _layer_prompt(x, p, cos_t, sin_t):
    nb, s, d = x.shape
    tm = _tile(s, 256)
    tkc = tm
    (hq, ckv, ckvb, kr, krp, qb, kb, kbb, vb, vbt, qc, kc, kcp, vc, vctp) = _in_proj(
        x, cos_t, sin_t, p["g_attn"], p["g_q"], p["g_kv"], p["w_ext"], tm=tm, tkc=tkc)
    q_a = _q_proj_a(hq, cos_t, sin_t, p["w_uq_ext"], tm=_tile(s, 512))
    k_a, vt_a = _kv_proj_a(ckvb, krp, p["w_uk"], p["w_uv"], tm=_tile(s, 512))
    ta = _tile(s, 1024)
    mix_a = _attn_a(q_a, k_a, vt_a, p["g_a"], tq=ta, tk=ta // 2, n_past=0, nh=2)
    mix_b = _attn_b(qb, kbb, vbt, p["bias"], p["g_b"], hist0=0, lead=B_PAST // B_QTILE)
    mix_c = _attn_c(qc, kcp, vctp, p["g_c"], tq=_tile(s, 512), tk=tkc, n_past=0)
    x2, h2 = _out_proj(mix_a, mix_b, mix_c, x, p["w_out"], p["g_ffn"], tm=_tile(s, 512))
    conv0 = jnp.zeros((nb, CONV_W - 1, 2 * p["w_down"].shape[0]), F32)
    tf = _tile(s, 1024)
    x3, conv_new = _ffn(h2, x2, p["w_up"], p["w_conv"], p["b_conv"], p["w_down"], conv0, tm=tf, seq=tf)
    rows = min(B_PAST, s)
    state = (ckv, kr, kb[:, s - rows:].reshape(nb, rows, B_HEADS, B_DIM), vb[:, s - rows:].reshape(nb, rows, B_HEADS, B_DIM),
             kc.reshape(nb, s, C_HEADS, C_DIM), vc.reshape(nb, s, C_HEADS, C_DIM), conv_new)
    return x3, state


def _pad_rows(t, n):
    return jnp.pad(t, ((0, 0), (0, n - t.shape[1]), (0, 0)))


def _layer_sample(x, p, cos_t, sin_t, ckv_past, kr_past, bk_past, bv_past, ck_past, cv_past, conv_past):
    nb, s, d = x.shape
    t = nb * s
    n_past = ckv_past.shape[1]
    xf = x.reshape(1, t, d)
    tm = _tile(t, 256)
    (hq, ckv, _, kr, krp, qb, kb, _, vb, _, qc, kc, _, vc, _) = _in_proj(
        xf, cos_t, sin_t, p["g_attn"], p["g_q"], p["g_kv"], p["w_ext"], tm=tm, tkc=tm)
    q_a = _q_proj_a(hq, cos_t, sin_t, p["w_uq_ext"], tm=tm)
    per = lambda a: a.reshape(nb, s, a.shape[-1])
    ckv, kr, kb, vb, kc, vc = per(ckv), per(kr), per(kb), per(vb), per(kc), per(vc)
    tqa = 256
    ska = n_past + tqa
    ckv_all = _pad_rows(jnp.concatenate([ckv_past, ckv], axis=1), ska).astype(BF16)
    kr_all = jnp.concatenate([kr_past, kr], axis=1)
    krp_all = _pad_rows(jnp.pad(kr_all, ((0, 0), (0, 0), (0, 128 - A_ROPE))), ska).astype(BF16)
    k_a, vt_a = _kv_proj_a(ckv_all, krp_all, p["w_uk"], p["w_uv"], tm=128)
    mix_a = _attn_a(_pad_rows(per(q_a), tqa), k_a, vt_a, p["g_a"], tq=tqa, tk=tqa // 2, n_past=n_past, nh=2)
    tq = 128
    sk = n_past + tq
    kb_all = _pad_rows(jnp.concatenate([bk_past.reshape(nb, -1, 512), kb], axis=1), B_WIN).astype(BF16)
    vb_all = _pad_rows(jnp.concatenate([bv_past.reshape(nb, -1, 512), vb], axis=1), B_WIN).astype(BF16)
    mix_b = _attn_b(_pad_rows(per(qb), B_QTILE), kb_all, vb_all.transpose(0, 2, 1), p["bias"], p["g_b"],
                    hist0=bk_past.shape[1], lead=0)
    kc_all = _pad_rows(jnp.concatenate([ck_past.reshape(nb, -1, 512), kc], axis=1), sk)
    vc_all = _pad_rows(jnp.concatenate([cv_past.reshape(nb, -1, 512), vc], axis=1), sk)
    kcp = _permute_keys(kc_all, 128).astype(BF16)
    vctp = _permute_keys(vc_all, 128).astype(BF16).transpose(0, 2, 1)
    mix_c = _attn_c(_pad_rows(per(qc), tq), kcp, vctp, p["g_c"], tq=tq, tk=128, n_past=n_past)
    flat = lambda mx: mx[:, :, :s].transpose(1, 0, 2).reshape(1, 512, t)
    x2, h2 = _out_proj(flat(mix_a), flat(mix_b), flat(mix_c), xf, p["w_out"], p["g_ffn"], tm=tm)
    x3, conv_new = _ffn(h2, x2, p["w_up"], p["w_conv"], p["b_conv"], p["w_down"], conv_past, tm=t, seq=s)
    x3 = x3.reshape(nb, s, d)
    state = (ckv, kr, kb.reshape(nb, s, B_HEADS, B_DIM), vb.reshape(nb, s, B_HEADS, B_DIM),
             kc.reshape(nb, s, C_HEADS, C_DIM), vc.reshape(nb, s, C_HEADS, C_DIM), conv_new)
    return x3, state


def kernel(x_prompt, x_sample, cache_a_ckv, cache_a_krope, cache_b_k, cache_b_v, cache_c_k, cache_c_v, state_conv,
           g_attn, w_in, g_q, w_uq, g_kv, w_uk, w_uv, rel_bias, g_heads, w_out, g_ffn, w_up, w_conv, b_conv, w_down,
           g_final):
    depth = w_in.shape[0]
    n_tp = x_prompt.shape[1]
    nbs, n_ts, _ = x_sample.shape
    past_len = cache_c_k.shape[2]
    cos_p, sin_p = _rope_tables(jnp.arange(n_tp, dtype=jnp.int32))
    pos_s = jnp.tile(past_len + jnp.arange(n_ts, dtype=jnp.int32), nbs)
    cos_s, sin_s = _rope_tables(pos_s)

    xp, xs = x_prompt, x_sample
    p_layers, s_layers = [], []
    for l in range(depth):
        p = _prep_layer(g_attn[l], w_in[l], g_q[l], w_uq[l], g_kv[l], w_uk[l], w_uv[l], rel_bias[l], g_heads[l],
                        w_out[l], g_ffn[l], w_up[l], w_conv[l], b_conv[l], w_down[l])
        xp, p_st = _layer_prompt(xp, p, cos_p, sin_p)
        xs, s_st = _layer_sample(xs, p, cos_s, sin_s, cache_a_ckv[l], cache_a_krope[l], cache_b_k[l], cache_b_v[l],
                                 cache_c_k[l], cache_c_v[l], state_conv[l])
        p_layers.append(p_st)
        s_layers.append(s_st)

    gf = g_final.reshape(1, -1)
    y_prompt = _final_norm(xp, gf, tm=_tile(n_tp, 512))
    y_sample = _final_norm(xs, gf, tm=n_ts)
    p_out = [jnp.stack(t) for t in zip(*p_layers)]
    s_out = [jnp.stack(t) for t in zip(*s_layers)]
    return (y_prompt, y_sample, *p_out, *s_out)
```

```python
import functools
import math

import numpy as np
import jax
import jax.numpy as jnp
from jax import lax
from jax.experimental import pallas as pl
from jax.experimental.pallas import tpu as pltpu

F32 = jnp.float32
BF16 = jnp.bfloat16

EPS = 1e-6
CHUNK = 64
ROPE_THETA = 10000.0
A_HEADS, A_NOPE, A_ROPE, A_VDIM = 4, 128, 64, 128
A_QK_PAD = 256
B_HEADS, B_DIM, B_PREV_CHUNKS, B_REL_CLIP = 4, 128, 8, 128
B_PAST = B_PREV_CHUNKS * CHUNK
C_HEADS, C_DIM = 4, 128
HEAD_DIM = 128
CONV_W = 3
LOG2E = 1.4426950408889634
A_SCALE = 1.0 / math.sqrt(A_NOPE + A_ROPE)
B_SCALE = 1.0 / math.sqrt(B_DIM)
C_SCALE = 1.0 / math.sqrt(C_DIM)

SB_EXIT_LOG2 = 160.0

B_QTILE = 4 * CHUNK
B_WIN = (B_PREV_CHUNKS + 4) * CHUNK

FFN_CHUNK = 512

VMEM_LIMIT = 56 * 1024 * 1024


def _cparams(sem):
    return pltpu.CompilerParams(dimension_semantics=sem, vmem_limit_bytes=VMEM_LIMIT)


def _rms_rows(x, g):
    ms = jnp.mean(x * x, axis=-1, keepdims=True)
    return x * lax.rsqrt(ms + EPS) * g


def _rms_cols(o, g):
    ms = jnp.mean(o * o, axis=0, keepdims=True)
    return o * lax.rsqrt(ms + EPS) * g


_NT = (((1,), (1,)), ((), ()))
_TN = (((0,), (0,)), ((), ()))


def _resident(shape, index_map):
    return pl.BlockSpec(shape, index_map, pipeline_mode=pl.Buffered(1))


IN_COLS = 512 + 512 + 256 + 1536 + 1536


def _in_proj_kernel(x_ref, cos_ref, sin_ref, gattn_ref, gq_ref, gkv_ref, w_ref,
                    hq_ref, ckv_ref, ckvb_ref, kr_ref, krp_ref,
                    qb_ref, kb_ref, kbb_ref, vb_ref, vbt_ref,
                    qc_ref, kc_ref, kcp_ref, vc_ref, vctp_ref, ks_ref, vs_ref, *, tkc):
    h = _rms_rows(x_ref[...], gattn_ref[...]).astype(BF16)

    def proj(lo, n):
        return jnp.dot(h, w_ref[:, lo:lo + n], preferred_element_type=F32)

    hq_ref[...] = _rms_rows(proj(0, 512), gq_ref[...]).astype(BF16)
    ckv = _rms_rows(proj(512, 512), gkv_ref[...])
    ckv_ref[...] = ckv
    ckvb_ref[...] = ckv.astype(BF16)
    kr2 = proj(1024, 256)
    krp = kr2[:, :128] * cos_ref[...] + kr2[:, 128:] * sin_ref[...]
    krp_ref[...] = krp.astype(BF16)
    kr_ref[...] = krp[:, :A_ROPE]
    o = 1280
    qb_ref[...] = (proj(o, 512) * (B_SCALE * LOG2E)).astype(BF16)
    kb = proj(o + 512, 512)
    kb_ref[...] = kb
    kbb_ref[...] = kb.astype(BF16)
    vb = proj(o + 1024, 512)
    vb_ref[...] = vb
    vbt_ref[...] = vb.T.astype(BF16)
    o = 2816
    qc_ref[...] = (proj(o, 512) * (C_SCALE * LOG2E)).astype(BF16)
    kc = proj(o + 512, 512)
    vc = proj(o + 1024, 512)
    kc_ref[...] = kc
    vc_ref[...] = vc
    tm = kc_ref.shape[0]
    m = tkc // 8
    for hd in range(C_HEADS):
        cols = slice(hd * C_DIM, (hd + 1) * C_DIM)
        ks_ref[hd] = kc[:, cols]
        vs_ref[hd] = vc[:, cols]
        for t0 in range(0, tm, tkc):
            kp = jnp.concatenate([ks_ref[hd, pl.ds(t0 + i, 8, stride=m), :] for i in range(m)], axis=0)
            kcp_ref[t0:t0 + tkc, cols] = kp.astype(BF16)
            vp = jnp.concatenate([vs_ref[hd, pl.ds(t0 + i, 8, stride=m), :] for i in range(m)], axis=0)
            vctp_ref[cols, t0:t0 + tkc] = vp.T.astype(BF16)


def _in_proj(x, cos_t, sin_t, g_attn, g_q, g_kv, w_ext, *, tm, tkc):
    nb, s, d = x.shape
    grid = (nb, s // tm)
    tok = lambda c, dt: jax.ShapeDtypeStruct((nb, s, c), dt)
    tr = lambda c: jax.ShapeDtypeStruct((nb, c, s), BF16)
    out_shape = (tok(512, BF16), tok(512, F32), tok(512, BF16), tok(A_ROPE, F32), tok(128, BF16),
                 tok(512, BF16), tok(512, F32), tok(512, BF16), tok(512, F32), tr(512),
                 tok(512, BF16), tok(512, F32), tok(512, BF16), tok(512, F32), tr(512))
    tspec = lambda c: pl.BlockSpec((None, tm, c), lambda b, i: (b, i, 0))
    trspec = pl.BlockSpec((None, 512, tm), lambda b, i: (b, 0, i))
    out_specs = (tspec(512), tspec(512), tspec(512), tspec(A_ROPE), tspec(128),
                 tspec(512), tspec(512), tspec(512), tspec(512), trspec,
                 tspec(512), tspec(512), tspec(512), tspec(512), trspec)
    vec = lambda n: _resident((1, n), lambda b, i: (0, 0))
    in_specs = [tspec(d),
                pl.BlockSpec((tm, 128), lambda b, i: (i, 0)),
                pl.BlockSpec((tm, 128), lambda b, i: (i, 0)),
                vec(d), vec(512), vec(512),
                _resident((d, IN_COLS), lambda b, i: (0, 0))]
    return pl.pallas_call(
        functools.partial(_in_proj_kernel, tkc=tkc),
        out_shape=out_shape, grid=grid, in_specs=in_specs, out_specs=out_specs,
        scratch_shapes=[pltpu.VMEM((C_HEADS, tm, C_DIM), F32), pltpu.VMEM((C_HEADS, tm, C_DIM), F32)],
        compiler_params=_cparams(("parallel", "parallel")), name="in_proj",
    )(x, cos_t, sin_t, g_attn, g_q, g_kv, w_ext)


def _q_proj_a_kernel(hq_ref, cos_ref, sin_ref, w_ref, q_ref):
    ql = jnp.dot(hq_ref[...], w_ref[...], preferred_element_type=F32)
    cos, sin = cos_ref[...], sin_ref[...]
    sc = A_SCALE * LOG2E
    for hd in range(A_HEADS):
        lo = hd * A_QK_PAD
        rot = ql[:, lo + 128:lo + 256] * cos + ql[:, 1024 + hd * 128:1024 + (hd + 1) * 128] * sin
        q_ref[:, lo:lo + 128] = (ql[:, lo:lo + 128] * sc).astype(BF16)
        q_ref[:, lo + 128:lo + 256] = (rot * sc).astype(BF16)


def _q_proj_a(hq, cos_t, sin_t, w_uq_ext, *, tm):
    nb, s, _ = hq.shape
    return pl.pallas_call(
        _q_proj_a_kernel,
        out_shape=jax.ShapeDtypeStruct((nb, s, A_HEADS * A_QK_PAD), BF16),
        grid=(nb, s // tm),
        in_specs=[pl.BlockSpec((None, tm, 512), lambda b, i: (b, i, 0)),
                  pl.BlockSpec((tm, 128), lambda b, i: (i, 0)),
                  pl.BlockSpec((tm, 128), lambda b, i: (i, 0)),
                  _resident((512, 1536), lambda b, i: (0, 0))],
        out_specs=pl.BlockSpec((None, tm, A_HEADS * A_QK_PAD), lambda b, i: (b, i, 0)),
        compiler_params=_cparams(("parallel", "parallel")), name="q_proj_a",
    )(hq, cos_t, sin_t, w_uq_ext)


def _kv_proj_a_kernel(ckv_ref, krp_ref, wk_ref, wv_ref, k_ref, vt_ref):
    c = ckv_ref[...]
    kn = jnp.dot(c, wk_ref[...], preferred_element_type=F32)
    v = jnp.dot(c, wv_ref[...], preferred_element_type=F32)
    krp = krp_ref[...]
    for hd in range(A_HEADS):
        lo = hd * A_QK_PAD
        k_ref[:, lo:lo + 128] = kn[:, hd * 128:(hd + 1) * 128].astype(BF16)
        k_ref[:, lo + 128:lo + 256] = krp
    vt_ref[...] = v.T.astype(BF16)


def _kv_proj_a(ckvb, krp, w_uk, w_uv, *, tm):
    nb, s, _ = ckvb.shape
    return pl.pallas_call(
        _kv_proj_a_kernel,
        out_shape=(jax.ShapeDtypeStruct((nb, s, A_HEADS * A_QK_PAD), BF16),
                   jax.ShapeDtypeStruct((nb, A_HEADS * A_VDIM, s), BF16)),
        grid=(nb, s // tm),
        in_specs=[pl.BlockSpec((None, tm, 512), lambda b, i: (b, i, 0)),
                  pl.BlockSpec((None, tm, 128), lambda b, i: (b, i, 0)),
                  _resident((512, 512), lambda b, i: (0, 0)),
                  _resident((512, 512), lambda b, i: (0, 0))],
        out_specs=(pl.BlockSpec((None, tm, A_HEADS * A_QK_PAD), lambda b, i: (b, i, 0)),
                   pl.BlockSpec((None, A_HEADS * A_VDIM, tm), lambda b, i: (b, 0, i))),
        compiler_params=_cparams(("parallel", "parallel")), name="kv_proj_a",
    )(ckvb, krp, w_uk, w_uv)


def _attn_a_kernel(q_ref, k_ref, vt_ref, g_ref, o_ref, s0_ref, s1_ref, m_ref, l_ref, acc_ref, *, tq, tk, n_past, nh):
    qi = pl.program_id(2)
    q0 = n_past + qi * tq
    n_full = q0 // tk
    m_ref[...] = jnp.full(m_ref.shape, -jnp.inf, F32)
    l_ref[...] = jnp.zeros(l_ref.shape, F32)
    acc_ref[...] = jnp.zeros(acc_ref.shape, F32)

    def scores(j, s_ref):
        start = pl.multiple_of(j * tk, tk)
        for hd in range(nh):
            qk = slice(hd * A_QK_PAD, (hd + 1) * A_QK_PAD)
            s_ref[hd] = lax.dot_general(k_ref[pl.ds(start, tk), qk], q_ref[:, qk], _NT, preferred_element_type=F32)

    def softmax_pv(j, s_ref, masked):
        start = pl.multiple_of(j * tk, tk)
        if masked:
            kpos = start + lax.broadcasted_iota(jnp.int32, (tk, tq), 0)
            qpos = q0 + lax.broadcasted_iota(jnp.int32, (tk, tq), 1)
            visible = (kpos >> 6) <= (qpos >> 6)
        for hd in range(nh):
            vv = slice(hd * A_VDIM, (hd + 1) * A_VDIM)
            s = s_ref[hd]
            if masked:
                s = jnp.where(visible, s, -jnp.inf)
            m_old = m_ref[hd]
            m_new = jnp.maximum(m_old, jnp.max(s, axis=0, keepdims=True))
            p = jnp.exp2(s - m_new)
            alpha = jnp.exp2(m_old - m_new)
            l_ref[hd] = alpha * l_ref[hd] + jnp.sum(p, axis=0, keepdims=True)
            pv = jnp.dot(vt_ref[vv, pl.ds(start, tk)], p.astype(BF16), preferred_element_type=F32)
            acc_ref[vv, :] = alpha * acc_ref[vv, :] + pv
            m_ref[hd] = m_new

    def pair_body(t, c):
        j = 2 * t
        scores(j + 1, s1_ref)
        softmax_pv(j, s0_ref, False)
        scores(j + 2, s0_ref)
        softmax_pv(j + 1, s1_ref, False)
        return c

    scores(0, s0_ref)
    lax.fori_loop(0, n_full // 2, pair_body, 0)
    scores(n_full + 1, s1_ref)
    softmax_pv(n_full, s0_ref, True)
    softmax_pv(n_full + 1, s1_ref, True)
    for hd in range(nh):
        vv = slice(hd * A_VDIM, (hd + 1) * A_VDIM)
        o = acc_ref[vv, :] / l_ref[hd]
        o_ref[vv, :] = _rms_cols(o, g_ref[vv, :]).astype(BF16)


def _attn_a(q, k, vt, g_col, *, tq, tk, n_past, nh):
    nb, sq, _ = q.shape
    sk = k.shape[1]
    assert tq == 2 * tk and n_past % tq == 0 and sq % tq == 0 and sk >= n_past + sq
    return pl.pallas_call(
        functools.partial(_attn_a_kernel, tq=tq, tk=tk, n_past=n_past, nh=nh),
        out_shape=jax.ShapeDtypeStruct((nb, A_HEADS * A_VDIM, sq), BF16),
        grid=(nb, A_HEADS // nh, sq // tq),
        in_specs=[pl.BlockSpec((None, tq, nh * A_QK_PAD), lambda b, h, i: (b, i, h)),
                  _resident((None, sk, nh * A_QK_PAD), lambda b, h, i: (b, 0, h)),
                  _resident((None, nh * A_VDIM, sk), lambda b, h, i: (b, h, 0)),
                  pl.BlockSpec((nh * A_VDIM, 1), lambda b, h, i: (h, 0))],
        out_specs=pl.BlockSpec((None, nh * A_VDIM, tq), lambda b, h, i: (b, h, i)),
        scratch_shapes=[pltpu.VMEM((nh, tk, tq), F32), pltpu.VMEM((nh, tk, tq), F32),
                        pltpu.VMEM((nh, 1, tq), F32), pltpu.VMEM((nh, 1, tq), F32),
                        pltpu.VMEM((nh * A_VDIM, tq), F32)],
        compiler_params=_cparams(("parallel", "parallel", "arbitrary")), name="attn_a",
    )(q, k, vt, g_col)


def _attn_b_kernel(q_ref, k0_ref, k1_ref, k2_ref, v0_ref, v1_ref, v2_ref, bias_ref, g_ref, o_ref, *, hist0):
    st = pl.program_id(1)
    thr = B_PAST - hist0 - st * B_QTILE
    row_ok = lax.broadcasted_iota(jnp.int32, (B_WIN, B_QTILE), 0) >= thr
    for hd in range(B_HEADS):
        cols = slice(hd * B_DIM, (hd + 1) * B_DIM)
        k = jnp.concatenate([k0_ref[:, cols], k1_ref[:, cols], k2_ref[:, cols]], axis=0)
        s = lax.dot_general(k, q_ref[:, cols], _NT, preferred_element_type=F32) + bias_ref[hd]
        s = jnp.where(row_ok, s, -jnp.inf)
        m = jnp.max(s, axis=0, keepdims=True)
        p = jnp.exp2(s - m)
        l = jnp.sum(p, axis=0, keepdims=True)
        v = jnp.concatenate([v0_ref[cols, :], v1_ref[cols, :], v2_ref[cols, :]], axis=1)
        o = jnp.dot(v, p.astype(BF16), preferred_element_type=F32) / l
        o_ref[cols, :] = _rms_cols(o, g_ref[cols, :]).astype(BF16)


def _attn_b(q, k_pad, vt_pad, bias, g_col, *, hist0, lead):
    nb, sq, w = q.shape
    kspec = lambda o: pl.BlockSpec((None, B_QTILE, w), lambda b, i: (b, jnp.maximum(i + o - lead, 0), 0))
    vspec = lambda o: pl.BlockSpec((None, w, B_QTILE), lambda b, i: (b, 0, jnp.maximum(i + o - lead, 0)))
    return pl.pallas_call(
        functools.partial(_attn_b_kernel, hist0=hist0),
        out_shape=jax.ShapeDtypeStruct((nb, w, sq), BF16),
        grid=(nb, sq // B_QTILE),
        in_specs=[pl.BlockSpec((None, B_QTILE, w), lambda b, i: (b, i, 0)),
                  kspec(0), kspec(1), kspec(2), vspec(0), vspec(1), vspec(2),
                  pl.BlockSpec((B_HEADS, B_WIN, B_QTILE), lambda b, i: (0, 0, 0)),
                  pl.BlockSpec((w, 1), lambda b, i: (0, 0))],
        out_specs=pl.BlockSpec((None, w, B_QTILE), lambda b, i: (b, 0, i)),
        compiler_params=_cparams(("parallel", "arbitrary")), name="attn_b",
    )(q, k_pad, k_pad, k_pad, vt_pad, vt_pad, vt_pad, bias, g_col)


def _band_bias_table(rel_bias):
    nh = rel_bias.shape[0]
    ring = 1024
    n_lo = B_WIN - 1 - (B_PAST + B_REL_CLIP)
    n_hi = B_PAST - B_REL_CLIP - 1
    lo = jnp.broadcast_to(rel_bias[:, :1], (nh, n_lo))
    hi = lambda n: jnp.broadcast_to(rel_bias[:, -1:], (nh, n))
    gap = ring - (B_QTILE + n_lo + rel_bias.shape[1] + n_hi)
    v = jnp.concatenate([hi(B_QTILE), jnp.zeros((nh, gap), rel_bias.dtype), lo, rel_bias, hi(n_hi)], axis=1)
    tbl = jnp.tile(v, (1, B_WIN))[:, :B_WIN * (ring - 1)].reshape(nh, B_WIN, ring - 1)[:, :, :B_QTILE] * LOG2E
    kc = np.arange(B_WIN)[:, None] // CHUNK
    qc = np.arange(B_QTILE)[None, :] // CHUNK
    valid = (kc >= qc) & (kc <= qc + B_PREV_CHUNKS)
    return jnp.where(valid[None], tbl, -jnp.inf)


def _attn_c_kernel(q_ref, kp_ref, vtp_ref, g_ref, o_ref, c_ref, acc_ref, *, tq, tk, n_past, nh):
    qi = pl.program_id(1)
    q0 = n_past + qi * tq
    n_full = q0 // tk
    n_mask = tq // tk
    m = tk // 8
    c_ref[...] = jnp.zeros(c_ref.shape, F32)
    acc_ref[...] = jnp.zeros(acc_ref.shape, F32)
    sub = lax.broadcasted_iota(jnp.int32, (8, tq), 0)

    def tile(j, masked):
        start = pl.multiple_of(j * tk, tk)
        zs = []
        for hd in range(nh):
            cols = slice(hd * C_DIM, (hd + 1) * C_DIM)
            zs.append(lax.dot_general(kp_ref[pl.ds(start, tk), cols], q_ref[:, cols], _NT,
                                      preferred_element_type=F32))
        if masked:
            r = lax.broadcasted_iota(jnp.int32, (tk, tq), 0)
            kpos = start + (r & 7) * m + (r >> 3)
            qpos = q0 + lax.broadcasted_iota(jnp.int32, (tk, tq), 1)
            causal = kpos < qpos
        cmin = None
        for hd in range(nh):
            cols = slice(hd * C_DIM, (hd + 1) * C_DIM)
            z = zs[hd]
            sp = jnp.maximum(z, 0.0) + jnp.log2(1.0 + jnp.exp2(-jnp.abs(z)))
            if masked:
                sp = jnp.where(causal, sp, 0.0)
            run = [None] * m
            run[m - 1] = sp[(m - 1) * 8:m * 8]
            for i in range(m - 2, -1, -1):
                run[i] = sp[i * 8:(i + 1) * 8] + run[i + 1]
            tot = run[0]
            incl = tot
            for sh in (1, 2, 4):
                incl = incl + jnp.where(sub + sh < 8, pltpu.roll(incl, 8 - sh, 0), 0.0)
            c_in = c_ref[hd]
            off = (incl - tot) + c_in
            rows = []
            for i in range(m):
                w = jnp.exp2(z[i * 8:(i + 1) * 8] - (run[i] + off))
                if masked:
                    w = jnp.where(causal[i * 8:(i + 1) * 8], w, 0.0)
                rows.append(w)
            w = jnp.concatenate(rows, axis=0).astype(BF16)
            acc_ref[cols, :] += jnp.dot(vtp_ref[cols, pl.ds(start, tk)], w, preferred_element_type=F32)
            c_new = c_in + incl[0:1, :]
            c_ref[hd] = c_new
            hmin = jnp.min(c_new)
            cmin = hmin if cmin is None else jnp.minimum(cmin, hmin)
        return cmin

    cmin = jnp.float32(0.0)
    for jj in range(n_mask - 1, -1, -1):
        cmin = tile(n_full + jj, True)

    def cond(carry):
        j, cm = carry
        return jnp.logical_and(j >= 0, cm < SB_EXIT_LOG2)

    def body(carry):
        j, _ = carry
        return j - 1, tile(j, False)

    lax.while_loop(cond, body, (n_full - 1, cmin))
    for hd in range(nh):
        cols = slice(hd * C_DIM, (hd + 1) * C_DIM)
        o_ref[cols, :] = _rms_cols(acc_ref[cols, :], g_ref[cols, :]).astype(BF16)


def _attn_c(q, kp, vtp, g_col, *, tq, tk, n_past):
    nb, sq, w = q.shape
    sk = kp.shape[1]
    nh = w // C_DIM
    return pl.pallas_call(
        functools.partial(_attn_c_kernel, tq=tq, tk=tk, n_past=n_past, nh=nh),
        out_shape=jax.ShapeDtypeStruct((nb, w, sq), BF16),
        grid=(nb, sq // tq),
        in_specs=[pl.BlockSpec((None, tq, w), lambda b, i: (b, i, 0)),
                  _resident((None, sk, w), lambda b, i: (b, 0, 0)),
                  _resident((None, w, sk), lambda b, i: (b, 0, 0)),
                  pl.BlockSpec((w, 1), lambda b, i: (0, 0))],
        out_specs=pl.BlockSpec((None, w, tq), lambda b, i: (b, 0, i)),
        scratch_shapes=[pltpu.VMEM((nh, 1, tq), F32), pltpu.VMEM((w, tq), F32)],
        compiler_params=_cparams(("parallel", "arbitrary")), name="attn_c",
    )(q, kp, vtp, g_col)


def _permute_keys(t, tk):
    nb, s, c = t.shape
    m = tk // 8
    return t.reshape(nb, s // tk, 8, m, c).transpose(0, 1, 3, 2, 4).reshape(nb, s, c)


def _out_proj_kernel(ma_ref, mb_ref, mc_ref, x_ref, w_ref, g_ref, x2_ref, h2_ref):
    y = lax.dot_general(ma_ref[...], w_ref[0:512, :], _TN, preferred_element_type=F32)
    y = y + lax.dot_general(mb_ref[...], w_ref[512:1024, :], _TN, preferred_element_type=F32)
    y = y + lax.dot_general(mc_ref[...], w_ref[1024:1536, :], _TN, preferred_element_type=F32)
    x2 = x_ref[...] + y
    x2_ref[...] = x2
    h2_ref[...] = _rms_rows(x2, g_ref[...]).astype(BF16)


def _out_proj(ma, mb, mc, x, w_out, g_ffn, *, tm):
    nb, s, d = x.shape
    mspec = pl.BlockSpec((None, 512, tm), lambda b, i: (b, 0, i))
    xspec = pl.BlockSpec((None, tm, d), lambda b, i: (b, i, 0))
    return pl.pallas_call(
        _out_proj_kernel,
        out_shape=(jax.ShapeDtypeStruct((nb, s, d), F32), jax.ShapeDtypeStruct((nb, s, d), BF16)),
        grid=(nb, s // tm),
        in_specs=[mspec, mspec, mspec, xspec,
                  _resident((1536, d), lambda b, i: (0, 0)),
                  _resident((1, d), lambda b, i: (0, 0))],
        out_specs=(xspec, xspec),
        compiler_params=_cparams(("parallel", "parallel")), name="out_proj",
    )(ma, mb, mc, x, w_out, g_ffn)


def _ffn_kernel(h_ref, x_ref, wa_ref, wg_ref, wca_ref, wcg_ref, ba_ref, bg_ref, wd_ref, pa_ref, pg_ref,
                o_ref, ca_ref, cg_ref, carry_ref, us_ref, *, seq, n_seq, fc):
    i = pl.program_id(1)
    f = pl.program_id(2)
    tm = n_seq * seq

    @pl.when(f == 0)
    def _():
        o_ref[...] = x_ref[...]

    if n_seq == 1:
        @pl.when(i == 0)
        def _():
            carry_ref[f, 0:2, :] = pa_ref[...]
            carry_ref[f, 2:4, :] = pg_ref[...]

    h = h_ref[...]
    hw = fc // 2
    r8 = lax.broadcasted_iota(jnp.int32, (8, hw), 0)

    def conv(slot, u, prev, wc, b):
        us_ref[slot, 6:8, :] = prev[0]
        us_ref[slot, 8:8 + tm, :] = u
        u1 = us_ref[slot, pl.ds(7, tm), :]
        u2 = us_ref[slot, pl.ds(6, tm), :]
        if n_seq > 1:
            p1, p2 = [], []
            for sq in range(n_seq):
                lo = sq * seq
                pr = prev[sq]
                p1 += [jnp.where(r8 == 0, pr[1:2, :], u1[lo:lo + 8]), u1[lo + 8:lo + seq]]
                p2 += [jnp.where(r8 == 0, pr[0:1, :], jnp.where(r8 == 1, pr[1:2, :], u2[lo:lo + 8])),
                       u2[lo + 8:lo + seq]]
            u1 = jnp.concatenate(p1, axis=0)
            u2 = jnp.concatenate(p2, axis=0)
        return b + wc[0:1, :] * u2 + wc[1:2, :] * u1 + wc[2:3, :] * u

    acts = []
    for half in range(2):
        cols = slice(half * hw, (half + 1) * hw)
        ua = jnp.dot(h, wa_ref[:, cols], preferred_element_type=F32)
        ug = jnp.dot(h, wg_ref[:, cols], preferred_element_type=F32)
        if n_seq == 1:
            prev_a = [carry_ref[f, 0:2, cols]]
            prev_g = [carry_ref[f, 2:4, cols]]
        else:
            prev_a = [pa_ref[sq, :, cols] for sq in range(n_seq)]
            prev_g = [pg_ref[sq, :, cols] for sq in range(n_seq)]
        a = conv(0, ua, prev_a, wca_ref[:, cols], ba_ref[:, cols])
        g = conv(1, ug, prev_g, wcg_ref[:, cols], bg_ref[:, cols])
        acts.append((a * (1.0 / (1.0 + jnp.exp(-a))) * g).astype(BF16))
        for sq in range(n_seq):
            hi = (sq + 1) * seq
            if n_seq == 1:
                carry_ref[f, 0:2, cols] = ua[hi - 2:hi, :]
                carry_ref[f, 2:4, cols] = ug[hi - 2:hi, :]
                ca_ref[:, cols] = ua[hi - 2:hi, :]
                cg_ref[:, cols] = ug[hi - 2:hi, :]
            else:
                ca_ref[sq, :, cols] = ua[hi - 2:hi, :]
                cg_ref[sq, :, cols] = ug[hi - 2:hi, :]
    o_ref[...] += jnp.dot(jnp.concatenate(acts, axis=1), wd_ref[...], preferred_element_type=F32)


def _ffn(h2, x2, w_up, w_conv, b_conv, w_down, conv_past, *, tm, seq):
    nb, s, d = x2.shape
    dff = w_down.shape[0]
    fc = w_up.shape[2]
    nf = dff // fc
    n_seq = tm // seq
    assert (n_seq == 1 or (nb == 1 and tm == s)) and fc % 256 == 0 and seq % 8 == 0 and seq > 8
    xspec = pl.BlockSpec((None, tm, d), lambda b, i, f: (b, i, 0))
    lo = lambda r: pl.BlockSpec((r, fc), lambda b, i, f: (0, f))
    hi = lambda r: pl.BlockSpec((r, fc), lambda b, i, f: (0, nf + f))
    up_lo = pl.BlockSpec((None, d, fc), lambda b, i, f: (f, 0, 0))
    up_hi = pl.BlockSpec((None, d, fc), lambda b, i, f: (nf + f, 0, 0))
    if n_seq == 1:
        st_lo = pl.BlockSpec((None, CONV_W - 1, fc), lambda b, i, f: (b, 0, f))
        st_hi = pl.BlockSpec((None, CONV_W - 1, fc), lambda b, i, f: (b, 0, nf + f))
        st_out = pl.BlockSpec((None, None, CONV_W - 1, fc), lambda b, i, f: (b, i, 0, f))
        st_shape = jax.ShapeDtypeStruct((nb, s // tm, CONV_W - 1, dff), F32)
    else:
        st_lo = pl.BlockSpec((n_seq, CONV_W - 1, fc), lambda b, i, f: (0, 0, f))
        st_hi = pl.BlockSpec((n_seq, CONV_W - 1, fc), lambda b, i, f: (0, 0, nf + f))
        st_out = st_lo
        st_shape = jax.ShapeDtypeStruct((n_seq, CONV_W - 1, dff), F32)
    x3, ca, cg = pl.pallas_call(
        functools.partial(_ffn_kernel, seq=seq, n_seq=n_seq, fc=fc),
        out_shape=(jax.ShapeDtypeStruct((nb, s, d), F32), st_shape, st_shape),
        grid=(nb, s // tm, nf),
        in_specs=[xspec, xspec, up_lo, up_hi, lo(CONV_W), hi(CONV_W), lo(1), hi(1),
                  pl.BlockSpec((fc, d), lambda b, i, f: (f, 0)), st_lo, st_hi],
        out_specs=(xspec, st_out, st_out),
        scratch_shapes=[pltpu.VMEM((nf, 4, fc), F32), pltpu.VMEM((2, tm + 8, fc // 2), F32)],
        compiler_params=_cparams(("parallel", "arbitrary", "arbitrary")), name="conv_ffn",
    )(h2, x2, w_up, w_up, w_conv, w_conv, b_conv, b_conv, w_down, conv_past, conv_past)
    if n_seq == 1:
        ca, cg = ca[:, -1], cg[:, -1]
    return x3, jnp.concatenate([ca, cg], axis=-1)


def _final_norm_kernel(x_ref, g_ref, o_ref):
    o_ref[...] = _rms_rows(x_ref[...], g_ref[...])


def _final_norm(x, g, *, tm):
    nb, s, d = x.shape
    spec = pl.BlockSpec((None, tm, d), lambda b, i: (b, i, 0))
    return pl.pallas_call(
        _final_norm_kernel, out_shape=jax.ShapeDtypeStruct(x.shape, F32), grid=(nb, s // tm),
        in_specs=[spec, pl.BlockSpec((1, d), lambda b, i: (0, 0))], out_specs=spec,
        compiler_params=_cparams(("parallel", "parallel")), name="final_norm",
    )(x, g)


def _rope_tables(pos):
    half = A_ROPE // 2
    inv = ROPE_THETA ** (-jnp.arange(half, dtype=F32) / half)
    ang = pos.astype(F32)[:, None] * inv[None, :]
    cos, sin = jnp.cos(ang), jnp.sin(ang)
    z = jnp.zeros((pos.shape[0], 128 - A_ROPE), F32)
    return jnp.concatenate([cos, cos, z], axis=1), jnp.concatenate([-sin, sin, z], axis=1)


def _chunk_major(w, fc):
    d, n = w.shape
    return w.reshape(d, n // fc, fc).transpose(1, 0, 2)


def _swap_halves_cols(w):
    half = w.shape[1] // 2
    return jnp.concatenate([w[:, half:], w[:, :half]], axis=1)


def _prep_layer(g_attn, w_in, g_q, w_uq, g_kv, w_uk, w_uv, rel_bias, g_heads, w_out, g_ffn, w_up, w_conv, b_conv,
                w_down):
    d = w_in.shape[0]
    kr = w_in[:, 1024:1024 + A_ROPE]
    z64 = jnp.zeros((d, 64), F32)
    w_ext = jnp.concatenate([w_in[:, :1024], kr, z64, _swap_halves_cols(kr), z64, w_in[:, 1024 + A_ROPE:]], axis=1)
    zq = jnp.zeros((w_uq.shape[0], 64), F32)
    main, swapped = [], []
    for hd in range(A_HEADS):
        lo = hd * (A_NOPE + A_ROPE)
        rope_cols = w_uq[:, lo + A_NOPE:lo + A_NOPE + A_ROPE]
        main += [w_uq[:, lo:lo + A_NOPE], rope_cols, zq]
        swapped += [_swap_halves_cols(rope_cols), zq]
    w_uq_ext = jnp.concatenate(main + swapped, axis=1)
    g_col = g_heads.reshape(-1, 1)
    return dict(
        g_attn=g_attn.reshape(1, -1), g_q=g_q.reshape(1, -1), g_kv=g_kv.reshape(1, -1),
        w_ext=w_ext.astype(BF16), w_uq_ext=w_uq_ext.astype(BF16), w_uk=w_uk.astype(BF16), w_uv=w_uv.astype(BF16),
        bias=_band_bias_table(rel_bias), g_a=g_col[0:512], g_b=g_col[512:1024], g_c=g_col[1024:1536],
        w_out=w_out.astype(BF16), g_ffn=g_ffn.reshape(1, -1), w_up=_chunk_major(w_up.astype(BF16), FFN_CHUNK), w_conv=w_conv,
        b_conv=b_conv.reshape(1, -1), w_down=w_down.astype(BF16))


def _tile(s, pref):
    t = min(s, pref)
    assert s % t == 0, (s, t)
    return t


def _layer_prompt(x, p, cos_t, sin_t):
    nb, s, d = x.shape
    tm = _tile(s, 256)
    tkc = tm
    (hq, ckv, ckvb, kr, krp, qb, kb, kbb, vb, vbt, qc, kc, kcp, vc, vctp) = _in_proj(
        x, cos_t, sin_t, p["g_attn"], p["g_q"], p["g_kv"], p["w_ext"], tm=tm, tkc=tkc)
    q_a = _q_proj_a(hq, cos_t, sin_t, p["w_uq_ext"], tm=_tile(s, 512))
    k_a, vt_a = _kv_proj_a(ckvb, krp, p["w_uk"], p["w_uv"], tm=_tile(s, 512))
    ta = _tile(s, 1024)
    mix_a = _attn_a(q_a, k_a, vt_a, p["g_a"], tq=ta, tk=ta // 2, n_past=0, nh=2)
    mix_b = _attn_b(qb, kbb, vbt, p["bias"], p["g_b"], hist0=0, lead=B_PAST // B_QTILE)
    mix_c = _attn_c(qc, kcp, vctp, p["g_c"], tq=_tile(s, 512), tk=tkc, n_past=0)
    x2, h2 = _out_proj(mix_a, mix_b, mix_c, x, p["w_out"], p["g_ffn"], tm=_tile(s, 512))
    conv0 = jnp.zeros((nb, CONV_W - 1, 2 * p["w_down"].shape[0]), F32)
    tf = _tile(s, 512)
    x3, conv_new = _ffn(h2, x2, p["w_up"], p["w_conv"], p["b_conv"], p["w_down"], conv0, tm=tf, seq=tf)
    rows = min(B_PAST, s)
    state = (ckv, kr, kb[:, s - rows:].reshape(nb, rows, B_HEADS, B_DIM), vb[:, s - rows:].reshape(nb, rows, B_HEADS, B_DIM),
             kc.reshape(nb, s, C_HEADS, C_DIM), vc.reshape(nb, s, C_HEADS, C_DIM), conv_new)
    return x3, state


def _pad_rows(t, n):
    return jnp.pad(t, ((0, 0), (0, n - t.shape[1]), (0, 0)))


def _layer_sample(x, p, cos_t, sin_t, ckv_past, kr_past, bk_past, bv_past, ck_past, cv_past, conv_past):
    nb, s, d = x.shape
    t = nb * s
    n_past = ckv_past.shape[1]
    xf = x.reshape(1, t, d)
    tm = _tile(t, 256)
    (hq, ckv, _, kr, krp, qb, kb, _, vb, _, qc, kc, _, vc, _) = _in_proj(
        xf, cos_t, sin_t, p["g_attn"], p["g_q"], p["g_kv"], p["w_ext"], tm=tm, tkc=tm)
    q_a = _q_proj_a(hq, cos_t, sin_t, p["w_uq_ext"], tm=tm)
    per = lambda a: a.reshape(nb, s, a.shape[-1])
    ckv, kr, kb, vb, kc, vc = per(ckv), per(kr), per(kb), per(vb), per(kc), per(vc)
    tqa = 256
    ska = n_past + tqa
    ckv_all = _pad_rows(jnp.concatenate([ckv_past, ckv], axis=1), ska).astype(BF16)
    kr_all = jnp.concatenate([kr_past, kr], axis=1)
    krp_all = _pad_rows(jnp.pad(kr_all, ((0, 0), (0, 0), (0, 128 - A_ROPE))), ska).astype(BF16)
    k_a, vt_a = _kv_proj_a(ckv_all, krp_all, p["w_uk"], p["w_uv"], tm=128)
    mix_a = _attn_a(_pad_rows(per(q_a), tqa), k_a, vt_a, p["g_a"], tq=tqa, tk=tqa // 2, n_past=n_past, nh=2)
    tq = 128
    sk = n_past + tq
    kb_all = _pad_rows(jnp.concatenate([bk_past.reshape(nb, -1, 512), kb], axis=1), B_WIN).astype(BF16)
    vb_all = _pad_rows(jnp.concatenate([bv_past.reshape(nb, -1, 512), vb], axis=1), B_WIN).astype(BF16)
    mix_b = _attn_b(_pad_rows(per(qb), B_QTILE), kb_all, vb_all.transpose(0, 2, 1), p["bias"], p["g_b"],
                    hist0=bk_past.shape[1], lead=0)
    kc_all = _pad_rows(jnp.concatenate([ck_past.reshape(nb, -1, 512), kc], axis=1), sk)
    vc_all = _pad_rows(jnp.concatenate([cv_past.reshape(nb, -1, 512), vc], axis=1), sk)
    kcp = _permute_keys(kc_all, 128).astype(BF16)
    vctp = _permute_keys(vc_all, 128).astype(BF16).transpose(0, 2, 1)
    mix_c = _attn_c(_pad_rows(per(qc), tq), kcp, vctp, p["g_c"], tq=tq, tk=128, n_past=n_past)
    flat = lambda mx: mx[:, :, :s].transpose(1, 0, 2).reshape(1, 512, t)
    x2, h2 = _out_proj(flat(mix_a), flat(mix_b), flat(mix_c), xf, p["w_out"], p["g_ffn"], tm=tm)
    x3, conv_new = _ffn(h2, x2, p["w_up"], p["w_conv"], p["b_conv"], p["w_down"], conv_past, tm=t, seq=s)
    x3 = x3.reshape(nb, s, d)
    state = (ckv, kr, kb.reshape(nb, s, B_HEADS, B_DIM), vb.reshape(nb, s, B_HEADS, B_DIM),
             kc.reshape(nb, s, C_HEADS, C_DIM), vc.reshape(nb, s, C_HEADS, C_DIM), conv_new)
    return x3, state


def kernel(x_prompt, x_sample, cache_a_ckv, cache_a_krope, cache_b_k, cache_b_v, cache_c_k, cache_c_v, state_conv,
           g_attn, w_in, g_q, w_uq, g_kv, w_uk, w_uv, rel_bias, g_heads, w_out, g_ffn, w_up, w_conv, b_conv, w_down,
           g_final):
    depth = w_in.shape[0]
    n_tp = x_prompt.shape[1]
    nbs, n_ts, _ = x_sample.shape
    past_len = cache_c_k.shape[2]
    cos_p, sin_p = _rope_tables(jnp.arange(n_tp, dtype=jnp.int32))
    pos_s = jnp.tile(past_len + jnp.arange(n_ts, dtype=jnp.int32), nbs)
    cos_s, sin_s = _rope_tables(pos_s)

    xp, xs = x_prompt, x_sample
    p_layers, s_layers = [], []
    for l in range(depth):
        p = _prep_layer(g_attn[l], w_in[l], g_q[l], w_uq[l], g_kv[l], w_uk[l], w_uv[l], rel_bias[l], g_heads[l],
                        w_out[l], g_ffn[l], w_up[l], w_conv[l], b_conv[l], w_down[l])
        xp, p_st = _layer_prompt(xp, p, cos_p, sin_p)
        xs, s_st = _layer_sample(xs, p, cos_s, sin_s, cache_a_ckv[l], cache_a_krope[l], cache_b_k[l], cache_b_v[l],
                                 cache_c_k[l], cache_c_v[l], state_conv[l])
        p_layers.append(p_st)
        s_layers.append(s_st)

    gf = g_final.reshape(1, -1)
    y_prompt = _final_norm(xp, gf, tm=_tile(n_tp, 512))
    y_sample = _final_norm(xs, gf, tm=n_ts)
    p_out = [jnp.stack(t) for t in zip(*p_layers)]
    s_out = [jnp.stack(t) for t in zip(*s_layers)]
    return (y_prompt, y_sample, *p_out, *s_out)
```
